```python
import jax
import jax.numpy as jnp
from jax import lax
import numpy as np


D_MODEL = 4096
BATCH = 2
SEQ = 8192
DEPTH = 4

GRID_W = 64
CTX_LEN = 256
N_MOD = 6
MOD_RANK = 512
D_FF = 4 * D_MODEL
N_MIXERS = 4
GROUP_W = D_MODEL // N_MIXERS
MIX_W = N_MIXERS * GROUP_W
NORM_HEAD = 256
EPS = 1e-6
NEG_INIT = -1e30

M_HEADS = 4
M_DV = GROUP_W // M_HEADS
M_DQK = M_DV // 2
M_CHUNK = 128
GATE_CAP = 15.0
S_GROUPS = 4
S_CHUNK = 128
G_HEADS = 4
G_DV = GROUP_W // G_HEADS
G_DK = G_DV // 2
G_RANK = 16
G_TAU = 16.0
G_CHUNK = 64
CONV_W = 3

SPLIT_SIZES = (M_HEADS * M_DQK, M_HEADS * M_DQK, GROUP_W, GROUP_W, 2 * 2 * M_HEADS,
               GROUP_W, GROUP_W,
               G_HEADS * G_DK, G_HEADS * G_DK, GROUP_W, GROUP_W, 2 * G_RANK,
               GROUP_W, GROUP_W, GROUP_W)
SPLIT_IDX = tuple(int(i) for i in np.cumsum(SPLIT_SIZES)[:-1])
D_IN = sum(SPLIT_SIZES)

kernel_name = 'hybrid_prefix_diffusion_trunk'


def rmsnorm(x, g):
    xf = x.astype(jnp.float32)
    xf = xf * lax.rsqrt(jnp.mean(jnp.square(xf), axis=-1, keepdims=True) + EPS)
    return (xf * g.astype(jnp.float32)).astype(x.dtype)


def group_norm(y, g, center):
    shp = y.shape
    yf = y.astype(jnp.float32).reshape(shp[:-1] + (shp[-1] // NORM_HEAD, NORM_HEAD))
    if center:
        yf = yf - jnp.mean(yf, axis=-1, keepdims=True)
    yf = yf * lax.rsqrt(jnp.mean(jnp.square(yf), axis=-1, keepdims=True) + EPS)
    return yf.reshape(shp) * g.astype(jnp.float32)


def modulation(cvec, a, b, bias):
    m = (jax.nn.silu(cvec) @ a) @ b + bias
    return m.reshape(cvec.shape[:-1] + (N_MOD, D_MODEL))


def modulate(h, g, shift, scale):
    return rmsnorm(h, g) * (1.0 + scale) + shift


def split_heads(a, n):
    b, t, _ = a.shape
    return a.reshape(b, t, n, -1).transpose(0, 2, 1, 3)


def merge_heads(a):
    b, n, t, d = a.shape
    return a.transpose(0, 2, 1, 3).reshape(b, t, n * d)


def to_chunks(a, L):
    a = a.reshape(a.shape[:2] + (a.shape[2] // L, L) + a.shape[3:])
    return jnp.moveaxis(a, 2, 0)


def from_chunks(a):
    a = jnp.moveaxis(a, 0, 2)
    return a.reshape(a.shape[:2] + (a.shape[2] * a.shape[3],) + a.shape[4:])


def mlstm_scan(inputs, state):
    q, k, v, ig, lf = inputs
    L = M_CHUNK
    mask = jnp.arange(L)[:, None] >= jnp.arange(L)[None, :]

    def step(carry, inp):
        C, n, m = carry
        qc, kc, vc, ic, fc = inp
        b = jnp.cumsum(fc, axis=-1)
        a_inter = b + m[..., None]
        d = jnp.where(mask, b[..., :, None] - b[..., None, :] + ic[..., None, :], -jnp.inf)
        m_t = jnp.maximum(a_inter, jnp.max(d, axis=-1))
        w_inter = jnp.exp(a_inter - m_t)
        s = jnp.einsum('bhtd,bhsd->bhts', qc, kc) * jnp.exp(d - m_t[..., None])
        num = (w_inter[..., None] * jnp.einsum('bhtd,bhde->bhte', qc, C)
               + jnp.einsum('bhts,bhse->bhte', s, vc))
        den = w_inter * jnp.einsum('bhtd,bhd->bht', qc, n) + jnp.sum(s, axis=-1)
        h = num / jnp.maximum(jnp.abs(den), jnp.exp(-m_t))[..., None]
        g_end = b[..., -1:] - b + ic
        m_new = jnp.maximum(b[..., -1] + m, jnp.max(g_end, axis=-1))
        w_old = jnp.exp(b[..., -1] + m - m_new)
        w_s = jnp.exp(g_end - m_new[..., None])
        C_new = w_old[..., None, None] * C + jnp.einsum('bhs,bhsd,bhse->bhde', w_s, kc, vc)
        n_new = w_old[..., None] * n + jnp.einsum('bhs,bhsd->bhd', w_s, kc)
        return (C_new, n_new, m_new), h

    xs = tuple(to_chunks(a, L) for a in inputs)
    state, hs = lax.scan(step, state, xs)
    return from_chunks(hs), state


def gla_scan(inputs, S0):
    q, k, v, la = inputs
    L = G_CHUNK
    mask = jnp.arange(L)[:, None] >= jnp.arange(L)[None, :]

    def step(S, inp):
        qc, kc, vc, lc = inp
        bc = jnp.cumsum(lc, axis=2)
        o_inter = jnp.einsum('bhtd,bhde->bhte', qc * jnp.exp(bc), S)
        diff = bc[:, :, :, None, :] - bc[:, :, None, :, :]
        decay = jnp.exp(jnp.where(mask[:, :, None], diff, -jnp.inf))
        att = jnp.einsum('bhtd,bhsd,bhtsd->bhts', qc, kc, decay)
        o = o_inter + jnp.einsum('bhts,bhse->bhte', att, vc)
        b_last = bc[:, :, -1]
        k_dec = kc * jnp.exp(b_last[:, :, None, :] - bc)
        S_new = jnp.exp(b_last)[..., None] * S + jnp.einsum('bhsd,bhse->bhde', k_dec, vc)
        return S_new, o

    xs = tuple(to_chunks(a, L) for a in inputs)
    S, os_ = lax.scan(step, S0, xs)
    return from_chunks(os_), S


def bidir_prefix_scan(scan_fn, init, ctx_fwd, ctx_bwd, lat_fwd, lat_bwd):
    flip = lambda xs: tuple(jnp.flip(a, axis=2) for a in xs)
    hc_f, st_f = scan_fn(ctx_fwd, init)
    hc_b, st_b = scan_fn(flip(ctx_bwd), init)
    hl_f, _ = scan_fn(lat_fwd, st_f)
    hl_b, _ = scan_fn(flip(lat_bwd), st_b)
    return hl_f + jnp.flip(hl_b, axis=2), hc_f + jnp.flip(hc_b, axis=2)


def mlstm_inputs(q, k, v, gates, gate_bias):
    bn, t, _ = q.shape
    qh = split_heads(q, M_HEADS).astype(jnp.float32) * (M_DQK ** -0.5)
    kh = split_heads(k, M_HEADS).astype(jnp.float32)
    vh = split_heads(v, M_HEADS).astype(jnp.float32)
    g = (gates + gate_bias).astype(jnp.float32)
    g = GATE_CAP * jnp.tanh(g / GATE_CAP)
    g = g.reshape(bn, t, 2, 2, M_HEADS).transpose(2, 3, 0, 4, 1)
    fwd = (qh, kh, vh, g[0, 0], jax.nn.log_sigmoid(g[0, 1]))
    bwd = (qh, kh, vh, g[1, 0], jax.nn.log_sigmoid(g[1, 1]))
    return fwd, bwd


def gla_inputs(q, k, v, a_lr, w_a2, b_a):
    bn, t, _ = q.shape
    qh = split_heads(q, G_HEADS).astype(jnp.float32) * (G_DK ** -0.5)
    kh = split_heads(k, G_HEADS).astype(jnp.float32)
    vh = split_heads(v, G_HEADS).astype(jnp.float32)
    pre = jnp.einsum('btdr,drk->dbtk', a_lr.reshape(bn, t, 2, G_RANK), w_a2) + b_a[:, None, None, :]
    la = jax.nn.log_sigmoid(pre.astype(jnp.float32)) / G_TAU
    fwd = (qh, kh, vh, split_heads(la[0], G_HEADS))
    bwd = (qh, kh, vh, split_heads(la[1], G_HEADS))
    return fwd, bwd


def spatial_gating(u, v, ln_g, w_s, b_s):
    u = jax.nn.gelu(u)
    v = group_norm(jax.nn.gelu(v), ln_g, True)
    bn, t, ch = v.shape
    vr = v.reshape(bn, t // S_CHUNK, S_CHUNK, S_GROUPS, ch // S_GROUPS)
    s = jnp.einsum('gpq,bnqgc->bnpgc', w_s.astype(jnp.float32), vr) + b_s.T.astype(jnp.float32)[None, None, :, :, None]
    return u * s.reshape(bn, t, ch)


def short_conv(h, bg, cg, w, n_rows):
    y = cg * h
    bn, t, ch = y.shape
    y = y.reshape(bn * n_rows, t // n_rows, ch)
    y = lax.conv_general_dilated(y, w[:, None, :].astype(y.dtype), (1,), ((CONV_W // 2, CONV_W // 2),),
                                 dimension_numbers=('NWC', 'WIO', 'NWC'), feature_group_count=ch)
    return bg * y.reshape(bn, t, ch)


def token_mix(hl, hc, rows, w_in, gate_bias, w_a2, b_a, ln_g, w_s, b_s, conv_w, mix_g, w_out, with_ctx):
    bn = hl.shape[0]
    zl = jnp.split(hl @ w_in, SPLIT_IDX, axis=-1)
    zc = jnp.split(hc @ w_in, SPLIT_IDX, axis=-1)
    gm, gs, gg, gc = (mix_g[i * GROUP_W:(i + 1) * GROUP_W] for i in range(N_MIXERS))

    mf_l, mb_l = mlstm_inputs(zl[0], zl[1], zl[2], zl[4], gate_bias)
    mf_c, mb_c = mlstm_inputs(zc[0], zc[1], zc[2], zc[4], gate_bias)
    m_init = (jnp.zeros((bn, M_HEADS, M_DQK, M_DV), jnp.float32),
              jnp.zeros((bn, M_HEADS, M_DQK), jnp.float32),
              jnp.full((bn, M_HEADS), NEG_INIT, jnp.float32))
    hm_l, hm_c = bidir_prefix_scan(mlstm_scan, m_init, mf_c, mb_c, mf_l, mb_l)

    gf_l, gb_l = gla_inputs(zl[7], zl[8], zl[9], zl[11], w_a2, b_a)
    gf_c, gb_c = gla_inputs(zc[7], zc[8], zc[9], zc[11], w_a2, b_a)
    g_init = jnp.zeros((bn, G_HEADS, G_DK, G_DV), jnp.float32)
    hg_l, hg_c = bidir_prefix_scan(gla_scan, g_init, gf_c, gb_c, gf_l, gb_l)

    def assemble(z, hm, hg, n_rows):
        m_out = group_norm(merge_heads(hm), gm, False) * jax.nn.sigmoid(z[3].astype(jnp.float32))
        s_out = group_norm(spatial_gating(z[5], z[6], ln_g, w_s, b_s), gs, False)
        g_out = group_norm(merge_heads(hg), gg, False) * jax.nn.silu(z[10].astype(jnp.float32))
        c_out = group_norm(short_conv(z[12], z[13], z[14], conv_w, n_rows), gc, False)
        y = jnp.concatenate([m_out, s_out, g_out, c_out], axis=-1).astype(z[0].dtype)
        return y @ w_out

    yl = assemble(zl, hm_l, hg_l, rows)
    yc = assemble(zc, hm_c, hg_c, 1) if with_ctx else None
    return yl, yc


def sqrelu_ffn(h, w1, w2):
    return jnp.square(jax.nn.relu(h @ w1)) @ w2


def setup_inputs(seed: int = 0) -> dict:
    key = jax.random.key(seed)
    ks = jax.random.split(key, 22)

    def nrm(k, shape, scale):
        return jax.random.normal(k, shape, jnp.float32) * scale

    f_bias = jnp.zeros((2, 2, M_HEADS), jnp.float32).at[:, 1, :].set(jnp.linspace(3.0, 6.0, M_HEADS))
    return {
        'x': nrm(ks[0], (BATCH, SEQ, D_MODEL), 1.0),
        'c': nrm(ks[1], (BATCH, D_MODEL), 1.0),
        'ctx': nrm(ks[2], (BATCH, CTX_LEN, D_MODEL), 1.0),
        'c_ctx': nrm(ks[3], (D_MODEL,), 1.0),
        'norm1_g': 1.0 + nrm(ks[4], (DEPTH, D_MODEL), 0.02),
        'norm2_g': 1.0 + nrm(ks[5], (DEPTH, D_MODEL), 0.02),
        'mod_a': nrm(ks[6], (DEPTH, D_MODEL, MOD_RANK), D_MODEL ** -0.5),
        'mod_b': nrm(ks[7], (DEPTH, MOD_RANK, N_MOD * D_MODEL), MOD_RANK ** -0.5),
        'mod_bias': nrm(ks[8], (DEPTH, N_MOD * D_MODEL), 0.02),
        'w_in': nrm(ks[9], (DEPTH, D_MODEL, D_IN), D_MODEL ** -0.5),
        'mlstm_gate_bias': f_bias.reshape(1, -1) + nrm(ks[10], (DEPTH, 4 * M_HEADS), 0.1),
        'gla_w_a2': nrm(ks[11], (DEPTH, 2, G_RANK, G_HEADS * G_DK), G_RANK ** -0.5),
        'gla_b_a': nrm(ks[12], (DEPTH, 2, G_HEADS * G_DK), 0.1),
        'sgu_ln_g': 1.0 + nrm(ks[13], (DEPTH, GROUP_W), 0.02),
        'sgu_w': nrm(ks[14], (DEPTH, S_GROUPS, S_CHUNK, S_CHUNK), S_CHUNK ** -0.5),
        'sgu_b': 1.0 + nrm(ks[15], (DEPTH, S_GROUPS, S_CHUNK), 0.1),
        'conv_w': nrm(ks[16], (DEPTH, CONV_W, GROUP_W), CONV_W ** -0.5),
        'mix_norm_g': 1.0 + nrm(ks[17], (DEPTH, MIX_W), 0.02),
        'w_out': nrm(ks[18], (DEPTH, MIX_W, D_MODEL), MIX_W ** -0.5),
        'w_ff1': nrm(ks[19], (DEPTH, D_MODEL, D_FF), D_MODEL ** -0.5),
        'w_ff2': nrm(ks[20], (DEPTH, D_FF, D_MODEL), D_FF ** -0.5),
        'final_norm_g': 1.0 + nrm(ks[21], (D_MODEL,), 0.02),
    }


def reference(x, c, ctx, c_ctx, norm1_g, norm2_g, mod_a, mod_b, mod_bias, w_in, mlstm_gate_bias,
              gla_w_a2, gla_b_a, sgu_ln_g, sgu_w, sgu_b, conv_w, mix_norm_g, w_out, w_ff1, w_ff2,
              final_norm_g):
    rows = x.shape[1] // GRID_W
    for l in range(DEPTH):
        last = l == DEPTH - 1
        ml = modulation(c, mod_a[l], mod_b[l], mod_bias[l])
        mc = modulation(c_ctx, mod_a[l], mod_b[l], mod_bias[l])
        sh1, sc1, gt1, sh2, sc2, gt2 = (ml[:, i, None, :] for i in range(N_MOD))
        csh1, csc1, cgt1, csh2, csc2, cgt2 = (mc[i] for i in range(N_MOD))
        hl = modulate(x, norm1_g[l], sh1, sc1)
        hc = modulate(ctx, norm1_g[l], csh1, csc1)
        yl, yc = token_mix(hl, hc, rows, w_in[l], mlstm_gate_bias[l], gla_w_a2[l], gla_b_a[l],
                           sgu_ln_g[l], sgu_w[l], sgu_b[l], conv_w[l], mix_norm_g[l], w_out[l],
                           not last)
        x = x + gt1 * yl
        x = x + gt2 * sqrelu_ffn(modulate(x, norm2_g[l], sh2, sc2), w_ff1[l], w_ff2[l])
        if not last:
            ctx = ctx + cgt1 * yc
            ctx = ctx + cgt2 * sqrelu_ffn(modulate(ctx, norm2_g[l], csh2, csc2), w_ff1[l], w_ff2[l])
    return rmsnorm(x, final_norm_g)
```

```python
import functools

import numpy as np
import jax
import jax.numpy as jnp
from jax import lax
from jax.experimental import pallas as pl
from jax.experimental.pallas import tpu as pltpu

F32 = jnp.float32
BF16 = jnp.bfloat16

GRID_W = 64
N_MOD = 6
N_MIXERS = 4
GROUP_W = 1024
NORM_HEAD = 256
EPS = 1e-6
NEG_INIT = -1e30

M_HEADS = 4
M_DV = GROUP_W // M_HEADS
M_DQK = M_DV // 2
M_CHUNK = 128
GATE_CAP = 15.0
S_GROUPS = 4
S_CHUNK = 128
G_HEADS = 4
G_DV = GROUP_W // G_HEADS
G_DK = G_DV // 2
G_RANK = 16
G_TAU = 16.0
G_CHUNK = 128
CONV_W = 3

_SPLIT_SIZES = (M_HEADS * M_DQK, M_HEADS * M_DQK, GROUP_W, GROUP_W, 2 * 2 * M_HEADS,
                GROUP_W, GROUP_W,
                G_HEADS * G_DK, G_HEADS * G_DK, GROUP_W, GROUP_W, 2 * G_RANK,
                GROUP_W, GROUP_W, GROUP_W)
_SPLIT_OFF = tuple(int(i) for i in np.concatenate([[0], np.cumsum(_SPLIT_SIZES)]))
_SEG_ORDER = (0, 1, 2, 3, 5, 6, 7, 8, 9, 10, 12, 13, 14, 4, 11)
Z_WIDE = 11264
Z_COLS = 11520
Z_SMALL_BLK = Z_WIDE // 128
GATE_COL0 = 0
ALR_COL0 = 2 * 2 * M_HEADS

VMEM_LIMIT = 56 * 1024 * 1024


def _cparams(sem, vmem=VMEM_LIMIT):
    return pltpu.CompilerParams(dimension_semantics=sem, vmem_limit_bytes=vmem)


def _log_sigmoid(x):
    return jnp.minimum(x, 0.0) - jnp.log1p(jnp.exp(-jnp.abs(x)))


def _split3(x):
    hi = x.astype(BF16)
    r1 = x - hi.astype(F32)
    mid = r1.astype(BF16)
    lo = (r1 - mid.astype(F32)).astype(BF16)
    return hi, mid, lo


def _dot(a, b):
    return jnp.dot(a, b, preferred_element_type=F32)


def _dot_nt(a, b):
    return lax.dot_general(a, b, (((1,), (1,)), ((), ())), preferred_element_type=F32)


def _dot_tn(a, b):
    return lax.dot_general(a, b, (((0,), (0,)), ((), ())), preferred_element_type=F32)


def _head_rms(y, g):
    outs = []
    for h in range(y.shape[-1] // NORM_HEAD):
        yh = y[:, h * NORM_HEAD:(h + 1) * NORM_HEAD]
        ms = jnp.mean(yh * yh, axis=-1, keepdims=True)
        outs.append(yh * lax.rsqrt(ms + EPS))
    return jnp.concatenate(outs, axis=-1) * g


def _mod_kernel(c_ref, a_ref, b_ref, bias_ref, o_ref, t_ref):
    @pl.when(pl.program_id(1) == 0)
    def _():
        cv = c_ref[...]
        s = cv * jax.nn.sigmoid(cv)
        t_ref[...] = _dot(s.astype(BF16), a_ref[0].astype(BF16))

    o_ref[0] = _dot(t_ref[...].astype(BF16), b_ref[0].astype(BF16)) + bias_ref[0]


def _modulation_all(cvec, mod_a, mod_b, mod_bias, tn=2048):
    depth, d, rank = mod_a.shape
    n = mod_b.shape[-1]
    return pl.pallas_call(
        _mod_kernel,
        out_shape=jax.ShapeDtypeStruct((depth, 8, n), F32),
        grid=(depth, n // tn),
        in_specs=[pl.BlockSpec((8, d), lambda l, j: (0, 0)),
                  pl.BlockSpec((1, d, rank), lambda l, j: (l, 0, 0)),
                  pl.BlockSpec((1, rank, tn), lambda l, j: (l, 0, j)),
                  pl.BlockSpec((1, 1, tn), lambda l, j: (l, 0, j))],
        out_specs=pl.BlockSpec((1, 8, tn), lambda l, j: (l, 0, j)),
        scratch_shapes=[pltpu.VMEM((8, rank), F32)],
        compiler_params=_cparams(("arbitrary", "arbitrary")),
        name="modulation",
    )(cvec, mod_a, mod_b, mod_bias.reshape(depth, 1, n))


def _modulate_kernel(x_ref, g_ref, m_ref, o_ref, *, shift_row, scale_row):
    x = x_ref[...]
    ms = jnp.mean(x * x, axis=-1, keepdims=True)
    xn = x * lax.rsqrt(ms + EPS) * g_ref[...]
    sc = m_ref[0, scale_row:scale_row + 1, :]
    sh = m_ref[0, shift_row:shift_row + 1, :]
    o_ref[...] = (xn * (1.0 + sc) + sh).astype(o_ref.dtype)


def _modulate(x, g, modt, geo, *, shift_row, scale_row, tr=256):
    rows, d = x.shape
    return pl.pallas_call(
        functools.partial(_modulate_kernel, shift_row=shift_row, scale_row=scale_row),
        out_shape=jax.ShapeDtypeStruct((rows, d), BF16),
        grid=(rows // tr,),
        in_specs=[pl.BlockSpec((tr, d), lambda i: (i, 0)),
                  pl.BlockSpec((1, d), lambda i: (0, 0)),
                  pl.BlockSpec((1, 8, d), lambda i: (geo.group(i * tr), 0, 0))],
        out_specs=pl.BlockSpec((tr, d), lambda i: (i, 0)),
        compiler_params=_cparams(("parallel",)),
        name="modulate",
    )(x, g.reshape(1, d), modt)


def _final_norm_kernel(x_ref, g_ref, o_ref):
    x = x_ref[...]
    ms = jnp.mean(x * x, axis=-1, keepdims=True)
    o_ref[...] = x * lax.rsqrt(ms + EPS) * g_ref[...]


def _final_norm(x, g, geo, tr=256):
    rows, d = x.shape
    skip = geo.ctx_rows // tr
    return pl.pallas_call(
        _final_norm_kernel,
        out_shape=jax.ShapeDtypeStruct((rows - geo.ctx_rows, d), F32),
        grid=((rows - geo.ctx_rows) // tr,),
        in_specs=[pl.BlockSpec((tr, d), lambda i: (i + skip, 0)),
                  pl.BlockSpec((1, d), lambda i: (0, 0))],
        out_specs=pl.BlockSpec((tr, d), lambda i: (i, 0)),
        compiler_params=_cparams(("parallel",)),
        name="final_norm",
    )(x, g.reshape(1, d))


def _mm_kernel(*refs, n_a, epi, nk, gate_row):
    a_refs = refs[:n_a]
    b_ref = refs[n_a]
    pos = n_a + 1
    if epi == "resid":
        x_ref, m_ref = refs[pos], refs[pos + 1]
        pos += 2
    o_ref = refs[pos]
    acc_ref = refs[pos + 1] if nk > 1 else None

    part = None
    off = 0
    for a in a_refs:
        kk = a.shape[1]
        p = _dot(a[...], b_ref[off:off + kk, :])
        part = p if part is None else part + p
        off += kk

    def finish(acc):
        if epi == "relu2":
            r = jnp.maximum(acc, 0.0)
            acc = r * r
        elif epi == "resid":
            acc = x_ref[...] + m_ref[0, gate_row:gate_row + 1, :] * acc
        o_ref[...] = acc.astype(o_ref.dtype)

    if nk == 1:
        finish(part)
    else:
        k = pl.program_id(2)

        @pl.when(k == 0)
        def _():
            acc_ref[...] = part

        @pl.when(k > 0)
        def _():
            acc_ref[...] += part

        @pl.when(k == nk - 1)
        def _():
            finish(acc_ref[...])


def _matmul(a_list, b, geo, *, out_dtype, epi="none", x=None, modt=None, gate_row=0,
            tm=512, tn=1024, tk=None, name="matmul"):
    rows = a_list[0].shape[0]
    ktot, n = b.shape
    widths = [a.shape[1] for a in a_list]
    assert sum(widths) == ktot
    if tk is None:
        tk = ktot
    nk = ktot // tk
    assert nk == 1 or len(a_list) == 1
    in_specs = []
    for w in widths:
        wk = w if nk == 1 else tk
        in_specs.append(pl.BlockSpec((tm, wk), lambda i, j, k: (i, k)))
    in_specs.append(pl.BlockSpec((tk, tn), lambda i, j, k: (k, j)))
    args = list(a_list) + [b]
    if epi == "resid":
        in_specs.append(pl.BlockSpec((tm, tn), lambda i, j, k: (i, j)))
        in_specs.append(pl.BlockSpec((1, 8, tn), lambda i, j, k: (geo.group(i * tm), 0, j)))
        args += [x, modt]
    scratch = [pltpu.VMEM((tm, tn), F32)] if nk > 1 else []
    return pl.pallas_call(
        functools.partial(_mm_kernel, n_a=len(a_list), epi=epi, nk=nk, gate_row=gate_row),
        out_shape=jax.ShapeDtypeStruct((rows, n), out_dtype),
        grid=(rows // tm, n // tn, nk),
        in_specs=in_specs,
        out_specs=pl.BlockSpec((tm, tn), lambda i, j, k: (i, j)),
        scratch_shapes=scratch,
        compiler_params=_cparams(("parallel", "parallel", "arbitrary")),
        name=name,
    )(*args)


class _Geo:
    def __init__(self, batch, ctx_len, seq):
        self.batch, self.ctx_len, self.seq = batch, ctx_len, seq
        self.ctx_rows = batch * ctx_len
        self.rows = self.ctx_rows + batch * seq

    def group(self, row0):
        return jnp.where(row0 < self.ctx_rows, 0, 1 + (row0 - self.ctx_rows) // self.seq)

    def scan_blocks(self, chunk):
        nc, nl = self.ctx_len // chunk, self.seq // chunk
        base = self.ctx_rows // chunk

        def fwd(b, j):
            return jnp.where(j < nc, b * nc + j, base + b * nl + (j - nc))

        def bwd(b, j):
            return jnp.where(j < nc, b * nc + (nc - 1 - j), base + b * nl + (nl - 1 - (j - nc)))

        return fwd, bwd, nc + nl


def _mlstm_kernel(qf, kf, vf, sf, qb, kb, vb, sb, bias_ref, tri_ref, of, ob,
                  c_ref, n_ref, m_ref):
    L = M_CHUNK

    @pl.when(pl.program_id(1) == 0)
    def _():
        c_ref[...] = jnp.zeros_like(c_ref)
        n_ref[...] = jnp.zeros_like(n_ref)
        m_ref[...] = jnp.full_like(m_ref, NEG_INIT)

    row = lax.broadcasted_iota(jnp.int32, (L, L), 0)
    col = lax.broadcasted_iota(jnp.int32, (L, L), 1)
    streams = ((qf, kf, vf, sf, of), (qb, kb, vb, sb, ob))
    for d, (q_ref, k_ref, v_ref, s_ref, o_ref) in enumerate(streams):
        mask = (row >= col) if d == 0 else (row <= col)
        g = s_ref[...] + bias_ref[...]
        g = GATE_CAP * jnp.tanh(g / GATE_CAP)
        lf = _log_sigmoid(g)
        hi, mid, lo = _split3(lf)
        cs = _dot(tri_ref[d], jnp.concatenate([hi, mid, lo], axis=1))
        bcum = cs[:, 0:128] + cs[:, 128:256] + cs[:, 256:384]
        bcum_t = bcum.T
        g_t = g.T
        last = L - 1 if d == 0 else 0
        for h in range(M_HEADS):
            sidx = d * M_HEADS + h
            ci = GATE_COL0 + d * 2 * M_HEADS + h
            cf = ci + M_HEADS
            b_col, b_row = bcum[:, cf:cf + 1], bcum_t[cf:cf + 1, :]
            i_col, i_row = g[:, ci:ci + 1], g_t[ci:ci + 1, :]
            b_tot = b_col[last:last + 1, :]
            q = q_ref[:, h * M_DQK:(h + 1) * M_DQK] * (M_DQK ** -0.5)
            k = k_ref[:, h * M_DQK:(h + 1) * M_DQK]
            v = v_ref[:, h * M_DV:(h + 1) * M_DV]
            qb16, kb16, vb16 = q.astype(BF16), k.astype(BF16), v.astype(BF16)
            c_st, n_st, m_st = c_ref[sidx], n_ref[sidx], m_ref[sidx][0:1, 0:1]

            a_inter = b_col + m_st
            dmat = jnp.where(mask, b_col - b_row + i_row, -jnp.inf)
            m_t = jnp.maximum(a_inter, jnp.max(dmat, axis=-1, keepdims=True))
            w_inter = jnp.exp(a_inter - m_t)
            s = _dot_nt(qb16, kb16) * jnp.exp(dmat - m_t)
            num = w_inter * _dot(qb16, c_st.astype(BF16)) + _dot(s.astype(BF16), vb16)
            den = (w_inter * jnp.sum(q * n_st, axis=-1, keepdims=True)
                   + jnp.sum(s, axis=-1, keepdims=True))
            o_ref[:, h * M_DV:(h + 1) * M_DV] = num / jnp.maximum(jnp.abs(den), jnp.exp(-m_t))

            g_end = b_tot - b_col + i_col
            m_new = jnp.maximum(b_tot + m_st, jnp.max(g_end, axis=0, keepdims=True))
            w_old = jnp.exp(b_tot + m_st - m_new)
            kw = k * jnp.exp(g_end - m_new)
            c_ref[sidx] = w_old * c_st + _dot_tn(kw.astype(BF16), vb16)
            n_ref[sidx] = w_old * n_st + jnp.sum(kw, axis=0, keepdims=True)
            m_ref[sidx] = jnp.broadcast_to(m_new, m_ref.shape[1:])


def _mlstm(z, gate_bias_row, tri, geo):
    L = M_CHUNK
    fwd, bwd, nsteps = geo.scan_blocks(L)
    qk_w = M_HEADS * M_DQK

    def specs(rb):
        return [pl.BlockSpec((L, qk_w), lambda b, j: (rb(b, j), 0)),
                pl.BlockSpec((L, qk_w), lambda b, j: (rb(b, j), 1)),
                pl.BlockSpec((L, GROUP_W), lambda b, j: (rb(b, j), 1)),
                pl.BlockSpec((L, 128), lambda b, j: (rb(b, j), Z_SMALL_BLK))]

    n_str = 2 * M_HEADS
    out = jax.ShapeDtypeStruct((geo.rows, GROUP_W), F32)
    return pl.pallas_call(
        _mlstm_kernel,
        out_shape=(out, out),
        grid=(geo.batch, nsteps),
        in_specs=specs(fwd) + specs(bwd) + [
            pl.BlockSpec((1, 128), lambda b, j: (0, 0)),
            pl.BlockSpec((2, L, L), lambda b, j: (0, 0, 0))],
        out_specs=(pl.BlockSpec((L, GROUP_W), lambda b, j: (fwd(b, j), 0)),
                   pl.BlockSpec((L, GROUP_W), lambda b, j: (bwd(b, j), 0))),
        scratch_shapes=[pltpu.VMEM((n_str, M_DQK, M_DV), F32),
                        pltpu.VMEM((n_str, 1, M_DQK), F32),
                        pltpu.VMEM((n_str, 8, 128), F32)],
        compiler_params=_cparams(("parallel", "arbitrary")),
        name="mlstm",
    )(z, z, z, z, z, z, z, z, gate_bias_row, tri)


def _gla_tables(L):
    nlev = int(np.log2(L))
    sel = np.zeros((nlev + 2, L, L), np.float32)
    msk = np.zeros((nlev, L, L), np.float32)
    t = np.arange(L)
    for lev in range(nlev):
        c = 1 << lev
        mid = (t // (2 * c)) * 2 * c + c
        upper = t >= mid
        for r in range(L):
            if upper[r]:
                sel[lev, r, mid[r]:r + 1] = 1.0
            else:
                sel[lev, r, r + 1:mid[r]] = 1.0
        same = (t[:, None] // (2 * c)) == (t[None, :] // (2 * c))
        msk[lev] = (upper[:, None] & ~upper[None, :] & same).astype(np.float32)
    sel[nlev] = (t[None, :] <= t[:, None]).astype(np.float32)
    sel[nlev + 1] = (t[None, :] > t[:, None]).astype(np.float32)
    sel2 = np.stack([sel, sel[:, ::-1, ::-1]]).reshape(2, (nlev + 2) * L, L)
    msk2 = np.stack([msk, msk[:, ::-1, ::-1]])
    return sel2, msk2


def _gla_kernel(qf, kf, vf, sf, qb, kb, vb, sb, w2_ref, ba_ref, sel_ref, msk_ref, of, ob,
                st_ref):
    L = G_CHUNK
    nlev = msk_ref.shape[1]
    hw = G_HEADS * G_DK

    @pl.when(pl.program_id(1) == 0)
    def _():
        st_ref[...] = jnp.zeros_like(st_ref)

    row = lax.broadcasted_iota(jnp.int32, (L, L), 0)
    col = lax.broadcasted_iota(jnp.int32, (L, L), 1)
    eye = row == col
    streams = ((qf, kf, vf, sf, of), (qb, kb, vb, sb, ob))
    for d, (q_ref, k_ref, v_ref, s_ref, o_ref) in enumerate(streams):
        pre = _dot(s_ref[...].astype(BF16), w2_ref[d]) + ba_ref[d]
        la = _log_sigmoid(pre) * (1.0 / G_TAU)
        hi = la.astype(BF16)
        lo = (la - hi.astype(F32)).astype(BF16)
        e2 = _dot(sel_ref[d], jnp.concatenate([hi, lo], axis=1))
        e_all = e2[:, :hw] + e2[:, hw:]
        w_lev = jnp.exp(e_all[:nlev * L])
        run = e_all[nlev * L:(nlev + 1) * L]
        rem = e_all[(nlev + 1) * L:]
        e_run, e_rem = jnp.exp(run), jnp.exp(rem)
        last = L - 1 if d == 0 else 0
        for h in range(G_HEADS):
            sidx = d * G_HEADS + h
            hs = slice(h * G_DK, (h + 1) * G_DK)
            q = q_ref[:, hs] * (G_DK ** -0.5)
            k = k_ref[:, hs]
            vb16 = v_ref[:, h * G_DV:(h + 1) * G_DV].astype(BF16)
            att = jnp.where(eye, jnp.sum(q * k, axis=-1, keepdims=True), 0.0)
            for lev in range(nlev):
                w = w_lev[lev * L:(lev + 1) * L, hs]
                a = _dot_nt((q * w).astype(BF16), (k * w).astype(BF16))
                att = att + a * msk_ref[d, lev]
            st = st_ref[sidx]
            o_ref[:, h * G_DV:(h + 1) * G_DV] = (
                _dot_nt((q * e_run[:, hs]).astype(BF16), st.astype(BF16))
                + _dot(att.astype(BF16), vb16))
            kd = (k * e_rem[:, hs]).astype(BF16)
            st_ref[sidx] = e_run[last:last + 1, hs] * st + _dot_tn(vb16, kd)


def _gla(z, w2pad, b_a, sel, msk, geo):
    L = G_CHUNK
    fwd, bwd, nsteps = geo.scan_blocks(L)
    qk_w = G_HEADS * G_DK
    q_blk = 5120 // qk_w
    v_blk = 6144 // GROUP_W

    def specs(rb):
        return [pl.BlockSpec((L, qk_w), lambda b, j: (rb(b, j), q_blk)),
                pl.BlockSpec((L, qk_w), lambda b, j: (rb(b, j), q_blk + 1)),
                pl.BlockSpec((L, GROUP_W), lambda b, j: (rb(b, j), v_blk)),
                pl.BlockSpec((L, 128), lambda b, j: (rb(b, j), Z_SMALL_BLK))]

    out = jax.ShapeDtypeStruct((geo.rows, GROUP_W), F32)
    return pl.pallas_call(
        _gla_kernel,
        out_shape=(out, out),
        grid=(geo.batch, nsteps),
        in_specs=specs(fwd) + specs(bwd) + [
            pl.BlockSpec(w2pad.shape, lambda b, j: (0, 0, 0)),
            pl.BlockSpec(b_a.shape, lambda b, j: (0, 0, 0)),
            pl.BlockSpec(sel.shape, lambda b, j: (0, 0, 0)),
            pl.BlockSpec(msk.shape, lambda b, j: (0, 0, 0, 0))],
        out_specs=(pl.BlockSpec((L, GROUP_W), lambda b, j: (fwd(b, j), 0)),
                   pl.BlockSpec((L, GROUP_W), lambda b, j: (bwd(b, j), 0))),
        scratch_shapes=[pltpu.VMEM((2 * G_HEADS, G_DV, G_DK), F32)],
        compiler_params=_cparams(("parallel", "arbitrary")),
        name="gla",
    )(z, z, z, z, z, z, z, z, w2pad, b_a, sel, msk)


def _combine_kernel(mf, mb, gf, gb, zo, zs, gm_ref, gg_ref, ym, yg):
    hm = _head_rms(mf[...] + mb[...], gm_ref[...])
    ym[...] = (hm * jax.nn.sigmoid(zo[...])).astype(ym.dtype)
    hg = _head_rms(gf[...] + gb[...], gg_ref[...])
    zg = zs[...]
    yg[...] = (hg * (zg * jax.nn.sigmoid(zg))).astype(yg.dtype)


def _combine(hm_f, hm_b, hg_f, hg_b, z, gm, gg, geo, tr=256):
    blk = lambda c: pl.BlockSpec((tr, GROUP_W), lambda i: (i, c))
    vec = pl.BlockSpec((1, GROUP_W), lambda i: (0, 0))
    out = jax.ShapeDtypeStruct((geo.rows, GROUP_W), BF16)
    return pl.pallas_call(
        _combine_kernel,
        out_shape=(out, out),
        grid=(geo.rows // tr,),
        in_specs=[blk(0), blk(0), blk(0), blk(0), blk(2), blk(7), vec, vec],
        out_specs=(blk(0), blk(0)),
        compiler_params=_cparams(("parallel",)),
        name="combine",
    )(hm_f, hm_b, hg_f, hg_b, z, z, gm.reshape(1, -1), gg.reshape(1, -1))


def _sgu_kernel(u_ref, v_ref, ln_ref, w_ref, b_ref, gs_ref, o_ref):
    u = jax.nn.gelu(u_ref[...])
    v = jax.nn.gelu(v_ref[...])
    gw = GROUP_W // S_GROUPS
    outs = []
    for g in range(S_GROUPS):
        vg = v[:, g * gw:(g + 1) * gw]
        parts = []
        for h in range(gw // NORM_HEAD):
            vh = vg[:, h * NORM_HEAD:(h + 1) * NORM_HEAD]
            vh = vh - jnp.mean(vh, axis=-1, keepdims=True)
            parts.append(vh * lax.rsqrt(jnp.mean(vh * vh, axis=-1, keepdims=True) + EPS))
        vn = (parts[0] if len(parts) == 1 else jnp.concatenate(parts, axis=-1))
        vn = vn * ln_ref[:, g * gw:(g + 1) * gw]
        s = _dot(w_ref[g], vn.astype(BF16)) + b_ref[:, g:g + 1]
        outs.append(u[:, g * gw:(g + 1) * gw] * s)
    o_ref[...] = _head_rms(jnp.concatenate(outs, axis=-1), gs_ref[...]).astype(o_ref.dtype)


def _sgu(z, ln_g, w_s16, b_s_t, gs, geo):
    L = S_CHUNK
    return pl.pallas_call(
        _sgu_kernel,
        out_shape=jax.ShapeDtypeStruct((geo.rows, GROUP_W), BF16),
        grid=(geo.rows // L,),
        in_specs=[pl.BlockSpec((L, GROUP_W), lambda i: (i, 3)),
                  pl.BlockSpec((L, GROUP_W), lambda i: (i, 4)),
                  pl.BlockSpec((1, GROUP_W), lambda i: (0, 0)),
                  pl.BlockSpec((S_GROUPS, L, L), lambda i: (0, 0, 0)),
                  pl.BlockSpec((L, S_GROUPS), lambda i: (0, 0)),
                  pl.BlockSpec((1, GROUP_W), lambda i: (0, 0))],
        out_specs=pl.BlockSpec((L, GROUP_W), lambda i: (i, 0)),
        compiler_params=_cparams(("parallel",)),
        name="sgu",
    )(z, z, ln_g.reshape(1, -1), w_s16, b_s_t, gs.reshape(1, -1))


def _conv_kernel(h_ref, bg_ref, cg_ref, w_ref, gc_ref, o_ref, *, tr, n_ctx_tiles, ctx_period):
    y = cg_ref[...] * h_ref[...]
    period = jnp.where(pl.program_id(0) < n_ctx_tiles, ctx_period, GRID_W)
    pos = lax.broadcasted_iota(jnp.int32, (tr, 1), 0) & (period - 1)
    prev = jnp.where(pos != 0, pltpu.roll(y, 1, axis=0), 0.0)
    nxt = jnp.where(pos != period - 1, pltpu.roll(y, tr - 1, axis=0), 0.0)
    c = w_ref[0:1, :] * prev + w_ref[1:2, :] * y + w_ref[2:3, :] * nxt
    o_ref[...] = _head_rms(bg_ref[...] * c, gc_ref[...]).astype(o_ref.dtype)


def _conv(z, conv_w, gc, geo):
    tr = min(geo.ctx_len, 256)
    assert geo.ctx_len % tr == 0 and tr % GRID_W == 0 and (tr & (tr - 1)) == 0
    blk = lambda c: pl.BlockSpec((tr, GROUP_W), lambda i: (i, c))
    return pl.pallas_call(
        functools.partial(_conv_kernel, tr=tr, n_ctx_tiles=geo.ctx_rows // tr, ctx_period=tr),
        out_shape=jax.ShapeDtypeStruct((geo.rows, GROUP_W), BF16),
        grid=(geo.rows // tr,),
        in_specs=[blk(8), blk(9), blk(10),
                  pl.BlockSpec((CONV_W, GROUP_W), lambda i: (0, 0)),
                  pl.BlockSpec((1, GROUP_W), lambda i: (0, 0))],
        out_specs=blk(0),
        compiler_params=_cparams(("parallel",)),
        name="short_conv",
    )(z, z, z, conv_w, gc.reshape(1, -1))


def _regroup_w_in(w_in):
    segs = [w_in[:, _SPLIT_OFF[i]:_SPLIT_OFF[i + 1]] for i in _SEG_ORDER]
    pad = jnp.zeros((w_in.shape[0], Z_COLS - sum(_SPLIT_SIZES)), w_in.dtype)
    return jnp.concatenate(segs + [pad], axis=1).astype(BF16)


def kernel(x, c, ctx, c_ctx, norm1_g, norm2_g, mod_a, mod_b, mod_bias, w_in, mlstm_gate_bias,
           gla_w_a2, gla_b_a, sgu_ln_g, sgu_w, sgu_b, conv_w, mix_norm_g, w_out, w_ff1, w_ff2,
           final_norm_g):
    batch, seq, d = x.shape
    ctx_len = ctx.shape[1]
    depth = mod_a.shape[0]
    assert ctx_len % max(M_CHUNK, G_CHUNK, S_CHUNK) == 0 and seq % 256 == 0 and d == N_MIXERS * GROUP_W
    geo = _Geo(batch, ctx_len, seq)

    cvec = jnp.zeros((8, d), F32).at[0].set(c_ctx).at[1:1 + batch].set(c)
    mods = _modulation_all(cvec, mod_a, mod_b, mod_bias)
    mods = mods.reshape(depth, 8, N_MOD, d)[:, :1 + batch]
    mods = jnp.pad(mods, ((0, 0), (0, 0), (0, 8 - N_MOD), (0, 0)))

    tri = jnp.asarray(np.stack([np.tril(np.ones((M_CHUNK, M_CHUNK), np.float32)),
                                np.triu(np.ones((M_CHUNK, M_CHUNK), np.float32))]), BF16)
    sel_np, msk_np = _gla_tables(G_CHUNK)
    sel = jnp.asarray(sel_np, BF16)
    msk = jnp.asarray(msk_np, F32)

    xs = jnp.concatenate([ctx.reshape(batch * ctx_len, d), x.reshape(batch * seq, d)], axis=0)
    tm = min(512, geo.ctx_rows)

    for l in range(depth):
        modt = mods[l]
        w_in16 = _regroup_w_in(w_in[l])
        gate_row = jnp.zeros((1, 128), F32).at[0, GATE_COL0:GATE_COL0 + 4 * M_HEADS].set(
            mlstm_gate_bias[l])
        w2pad = jnp.zeros((2, 128, G_HEADS * G_DK), F32)
        for dd in range(2):
            w2pad = w2pad.at[dd, ALR_COL0 + dd * G_RANK:ALR_COL0 + (dd + 1) * G_RANK].set(
                gla_w_a2[l, dd])
        gm, gs, gg, gc = (mix_norm_g[l, i * GROUP_W:(i + 1) * GROUP_W] for i in range(N_MIXERS))

        h1 = _modulate(xs, norm1_g[l], modt, geo, shift_row=0, scale_row=1)
        z = _matmul([h1], w_in16, geo, out_dtype=F32, tm=tm, tn=1280, name="mm_in")
        hm_f, hm_b = _mlstm(z, gate_row, tri, geo)
        hg_f, hg_b = _gla(z, w2pad.astype(BF16), gla_b_a[l].reshape(2, 1, -1), sel, msk, geo)
        y_m, y_g = _combine(hm_f, hm_b, hg_f, hg_b, z, gm, gg, geo)
        y_s = _sgu(z, sgu_ln_g[l], sgu_w[l].astype(BF16), sgu_b[l].T, gs, geo)
        y_c = _conv(z, conv_w[l], gc, geo)
        xs = _matmul([y_m, y_s, y_g, y_c], w_out[l].astype(BF16), geo, out_dtype=F32,
                     epi="resid", x=xs, modt=modt, gate_row=2, tm=tm, tn=1024, name="mm_out")
        h2 = _modulate(xs, norm2_g[l], modt, geo, shift_row=3, scale_row=4)
        hid = _matmul([h2], w_ff1[l].astype(BF16), geo, out_dtype=BF16, epi="relu2",
                      tm=tm, tn=1024, name="mm_ff1")
        xs = _matmul([hid], w_ff2[l].astype(BF16), geo, out_dtype=F32, epi="resid", x=xs,
                     modt=modt, gate_row=5, tm=tm, tn=2048, tk=2048, name="mm_ff2")

    out = _final_norm(xs, final_norm_g, geo)
    return out.reshape(batch, seq, d)
```

```python
import functools

import numpy as np
import jax
import jax.numpy as jnp
from jax import lax
from jax.experimental import pallas as pl
from jax.experimental.pallas import tpu as pltpu

F32 = jnp.float32
BF16 = jnp.bfloat16

GRID_W = 64
N_MOD = 6
N_MIXERS = 4
GROUP_W = 1024
NORM_HEAD = 256
EPS = 1e-6
NEG_INIT = -1e30

M_HEADS = 4
M_DV = GROUP_W // M_HEADS
M_DQK = M_DV // 2
M_CHUNK = 128
GATE_CAP = 15.0
S_GROUPS = 4
S_CHUNK = 128
G_HEADS = 4
G_DV = GROUP_W // G_HEADS
G_DK = G_DV // 2
G_RANK = 16
G_TAU = 16.0
G_CHUNK = 128
CONV_W = 3

_SPLIT_SIZES = (M_HEADS * M_DQK, M_HEADS * M_DQK, GROUP_W, GROUP_W, 2 * 2 * M_HEADS,
                GROUP_W, GROUP_W,
                G_HEADS * G_DK, G_HEADS * G_DK, GROUP_W, GROUP_W, 2 * G_RANK,
                GROUP_W, GROUP_W, GROUP_W)
_SPLIT_OFF = tuple(int(i) for i in np.concatenate([[0], np.cumsum(_SPLIT_SIZES)]))
_SEG_ORDER = (0, 1, 2, 3, 5, 6, 7, 8, 9, 10, 12, 13, 14, 4, 11)
Z_WIDE = 11264
Z_COLS = 11520
Z_SMALL_BLK = Z_WIDE // 128
GATE_COL0 = 0
ALR_COL0 = 2 * 2 * M_HEADS

VMEM_LIMIT = 56 * 1024 * 1024


def _cparams(sem, vmem=VMEM_LIMIT):
    return pltpu.CompilerParams(dimension_semantics=sem, vmem_limit_bytes=vmem)


def _log_sigmoid(x):
    return jnp.minimum(x, 0.0) - jnp.log1p(jnp.exp(-jnp.abs(x)))


def _split3(x):
    hi = x.astype(BF16)
    r1 = x - hi.astype(F32)
    mid = r1.astype(BF16)
    lo = (r1 - mid.astype(F32)).astype(BF16)
    return hi, mid, lo


def _dot(a, b):
    return jnp.dot(a, b, preferred_element_type=F32)


def _dot_nt(a, b):
    return lax.dot_general(a, b, (((1,), (1,)), ((), ())), preferred_element_type=F32)


def _dot_tn(a, b):
    return lax.dot_general(a, b, (((0,), (0,)), ((), ())), preferred_element_type=F32)


def _head_rms(y, g):
    outs = []
    for h in range(y.shape[-1] // NORM_HEAD):
        yh = y[:, h * NORM_HEAD:(h + 1) * NORM_HEAD]
        ms = jnp.mean(yh * yh, axis=-1, keepdims=True)
        outs.append(yh * lax.rsqrt(ms + EPS))
    return jnp.concatenate(outs, axis=-1) * g


def _mod_kernel(c_ref, a_ref, b_ref, bias_ref, o_ref, t_ref):
    @pl.when(pl.program_id(1) == 0)
    def _():
        cv = c_ref[...]
        s = cv * jax.nn.sigmoid(cv)
        t_ref[...] = _dot(s.astype(BF16), a_ref[0].astype(BF16))

    o_ref[0] = _dot(t_ref[...].astype(BF16), b_ref[0].astype(BF16)) + bias_ref[0]


def _modulation_all(cvec, mod_a, mod_b, mod_bias, tn=2048):
    depth, d, rank = mod_a.shape
    n = mod_b.shape[-1]
    return pl.pallas_call(
        _mod_kernel,
        out_shape=jax.ShapeDtypeStruct((depth, 8, n), F32),
        grid=(depth, n // tn),
        in_specs=[pl.BlockSpec((8, d), lambda l, j: (0, 0)),
                  pl.BlockSpec((1, d, rank), lambda l, j: (l, 0, 0)),
                  pl.BlockSpec((1, rank, tn), lambda l, j: (l, 0, j)),
                  pl.BlockSpec((1, 1, tn), lambda l, j: (l, 0, j))],
        out_specs=pl.BlockSpec((1, 8, tn), lambda l, j: (l, 0, j)),
        scratch_shapes=[pltpu.VMEM((8, rank), F32)],
        compiler_params=_cparams(("arbitrary", "arbitrary")),
        name="modulation",
    )(cvec, mod_a, mod_b, mod_bias.reshape(depth, 1, n))


def _modulate_kernel(x_ref, g_ref, m_ref, o_ref, *, shift_row, scale_row):
    x = x_ref[...]
    ms = jnp.mean(x * x, axis=-1, keepdims=True)
    xn = x * lax.rsqrt(ms + EPS) * g_ref[...]
    sc = m_ref[0, scale_row:scale_row + 1, :]
    sh = m_ref[0, shift_row:shift_row + 1, :]
    o_ref[...] = (xn * (1.0 + sc) + sh).astype(o_ref.dtype)


def _modulate(x, g, modt, geo, *, shift_row, scale_row, tr=256):
    rows, d = x.shape
    return pl.pallas_call(
        functools.partial(_modulate_kernel, shift_row=shift_row, scale_row=scale_row),
        out_shape=jax.ShapeDtypeStruct((rows, d), BF16),
        grid=(rows // tr,),
        in_specs=[pl.BlockSpec((tr, d), lambda i: (i, 0)),
                  pl.BlockSpec((1, d), lambda i: (0, 0)),
                  pl.BlockSpec((1, 8, d), lambda i: (geo.group(i * tr), 0, 0))],
        out_specs=pl.BlockSpec((tr, d), lambda i: (i, 0)),
        compiler_params=_cparams(("parallel",)),
        name="modulate",
    )(x, g.reshape(1, d), modt)


def _final_norm_kernel(x_ref, g_ref, o_ref):
    x = x_ref[...]
    ms = jnp.mean(x * x, axis=-1, keepdims=True)
    o_ref[...] = x * lax.rsqrt(ms + EPS) * g_ref[...]


def _final_norm(x, g, geo, tr=256):
    rows, d = x.shape
    skip = geo.ctx_rows // tr
    return pl.pallas_call(
        _final_norm_kernel,
        out_shape=jax.ShapeDtypeStruct((rows - geo.ctx_rows, d), F32),
        grid=((rows - geo.ctx_rows) // tr,),
        in_specs=[pl.BlockSpec((tr, d), lambda i: (i + skip, 0)),
                  pl.BlockSpec((1, d), lambda i: (0, 0))],
        out_specs=pl.BlockSpec((tr, d), lambda i: (i, 0)),
        compiler_params=_cparams(("parallel",)),
        name="final_norm",
    )(x, g.reshape(1, d))


def _mm_kernel(*refs, n_a, epi, nk, n_sub):
    a_refs = refs[:n_a]
    b_ref = refs[n_a]
    pos = n_a + 1
    if epi == "resid":
        x_ref, g_ref = refs[pos], refs[pos + 1]
        pos += 2
    o_ref = refs[pos]

    part = None
    off = 0
    for a in a_refs:
        kk = a.shape[1]
        p = _dot(a[...], b_ref[off:off + kk, :])
        part = p if part is None else part + p
        off += kk

    if epi == "relu2":
        r = jnp.maximum(part, 0.0)
        o_ref[...] = (r * r).astype(o_ref.dtype)
    elif epi == "resid":
        sub = part.shape[0] // n_sub
        gated = jnp.concatenate(
            [g_ref[r] * part[r * sub:(r + 1) * sub] for r in range(n_sub)], axis=0)
        if nk == 1:
            o_ref[...] = x_ref[...] + gated
        else:
            o_ref[...] = jnp.where(pl.program_id(2) == 0, x_ref[...], o_ref[...]) + gated
    else:
        o_ref[...] = part.astype(o_ref.dtype)


GATE_SUB = 256


def _matmul(a_list, b, *, out_dtype, epi="none", x=None, gate_tab=None,
            tm=768, tn=1024, tk=None, name="matmul"):
    rows = a_list[0].shape[0]
    ktot, n = b.shape
    widths = [a.shape[1] for a in a_list]
    assert sum(widths) == ktot and rows % tm == 0 and n % tn == 0 and tm % GATE_SUB == 0
    if tk is None:
        tk = ktot
    nk = ktot // tk
    assert ktot % tk == 0 and (nk == 1 or (len(a_list) == 1 and epi == "resid"))
    n_sub = tm // GATE_SUB
    in_specs = []
    for w in widths:
        wk = w if nk == 1 else tk
        in_specs.append(pl.BlockSpec((tm, wk), lambda i, j, k: (i, k)))
    in_specs.append(pl.BlockSpec((tk, tn), lambda i, j, k: (k, j)))
    args = list(a_list) + [b]
    if epi == "resid":
        in_specs.append(pl.BlockSpec((tm, tn), lambda i, j, k: (i, j)))
        in_specs.append(pl.BlockSpec((n_sub, 1, tn), lambda i, j, k: (i, 0, j)))
        args += [x, gate_tab]
    return pl.pallas_call(
        functools.partial(_mm_kernel, n_a=len(a_list), epi=epi, nk=nk, n_sub=n_sub),
        out_shape=jax.ShapeDtypeStruct((rows, n), out_dtype),
        grid=(rows // tm, n // tn, nk),
        in_specs=in_specs,
        out_specs=pl.BlockSpec((tm, tn), lambda i, j, k: (i, j)),
        compiler_params=_cparams(("parallel", "parallel", "arbitrary")),
        name=name,
    )(*args)


class _Geo:
    def __init__(self, batch, ctx_len, seq):
        self.batch, self.ctx_len, self.seq = batch, ctx_len, seq
        self.ctx_rows = batch * ctx_len
        self.rows = self.ctx_rows + batch * seq

    def group(self, row0):
        return jnp.where(row0 < self.ctx_rows, 0, 1 + (row0 - self.ctx_rows) // self.seq)

    def scan_blocks(self, chunk):
        nc, nl = self.ctx_len // chunk, self.seq // chunk
        base = self.ctx_rows // chunk

        def fwd(b, j):
            return jnp.where(j < nc, b * nc + j, base + b * nl + (j - nc))

        def bwd(b, j):
            return jnp.where(j < nc, b * nc + (nc - 1 - j), base + b * nl + (nl - 1 - (j - nc)))

        return fwd, bwd, nc + nl


def _mlstm_kernel(qf, kf, vf, sf, qb, kb, vb, sb, bias_ref, tri_ref, of, ob,
                  c_ref, n_ref, m_ref):
    L = M_CHUNK

    @pl.when(pl.program_id(1) == 0)
    def _():
        c_ref[...] = jnp.zeros_like(c_ref)
        n_ref[...] = jnp.zeros_like(n_ref)
        m_ref[...] = jnp.full_like(m_ref, NEG_INIT)

    row = lax.broadcasted_iota(jnp.int32, (L, L), 0)
    col = lax.broadcasted_iota(jnp.int32, (L, L), 1)
    streams = ((qf, kf, vf, sf, of), (qb, kb, vb, sb, ob))
    for d, (q_ref, k_ref, v_ref, s_ref, o_ref) in enumerate(streams):
        mask = (row >= col) if d == 0 else (row <= col)
        g = s_ref[...] + bias_ref[...]
        g = GATE_CAP * jnp.tanh(g / GATE_CAP)
        lf = _log_sigmoid(g)
        hi, mid, lo = _split3(lf)
        cs = _dot(tri_ref[d], jnp.concatenate([hi, mid, lo], axis=1))
        bcum = cs[:, 0:128] + cs[:, 128:256] + cs[:, 256:384]
        bcum_t = bcum.T
        g_t = g.T
        last = L - 1 if d == 0 else 0
        for h in range(M_HEADS):
            sidx = d * M_HEADS + h
            ci = GATE_COL0 + d * 2 * M_HEADS + h
            cf = ci + M_HEADS
            b_col, b_row = bcum[:, cf:cf + 1], bcum_t[cf:cf + 1, :]
            i_col, i_row = g[:, ci:ci + 1], g_t[ci:ci + 1, :]
            b_tot = b_col[last:last + 1, :]
            q = q_ref[:, h * M_DQK:(h + 1) * M_DQK] * (M_DQK ** -0.5)
            k = k_ref[:, h * M_DQK:(h + 1) * M_DQK]
            v = v_ref[:, h * M_DV:(h + 1) * M_DV]
            qb16, kb16, vb16 = q.astype(BF16), k.astype(BF16), v.astype(BF16)
            c_st, n_st, m_st = c_ref[sidx], n_ref[sidx], m_ref[sidx][0:1, 0:1]

            a_inter = b_col + m_st
            dmat = jnp.where(mask, b_col - b_row + i_row, -jnp.inf)
            m_t = jnp.maximum(a_inter, jnp.max(dmat, axis=-1, keepdims=True))
            w_inter = jnp.exp(a_inter - m_t)
            s = _dot_nt(qb16, kb16) * jnp.exp(dmat - m_t)
            num = w_inter * _dot(qb16, c_st.astype(BF16)) + _dot(s.astype(BF16), vb16)
            den = (w_inter * jnp.sum(q * n_st, axis=-1, keepdims=True)
                   + jnp.sum(s, axis=-1, keepdims=True))
            o_ref[:, h * M_DV:(h + 1) * M_DV] = num / jnp.maximum(jnp.abs(den), jnp.exp(-m_t))

            g_end = b_tot - b_col + i_col
            m_new = jnp.maximum(b_tot + m_st, jnp.max(g_end, axis=0, keepdims=True))
            w_old = jnp.exp(b_tot + m_st - m_new)
            kw = k * jnp.exp(g_end - m_new)
            c_ref[sidx] = w_old * c_st + _dot_tn(kw.astype(BF16), vb16)
            n_ref[sidx] = w_old * n_st + jnp.sum(kw, axis=0, keepdims=True)
            m_ref[sidx] = jnp.broadcast_to(m_new, m_ref.shape[1:])


def _mlstm(z, gate_bias_row, tri, geo):
    L = M_CHUNK
    fwd, bwd, nsteps = geo.scan_blocks(L)
    qk_w = M_HEADS * M_DQK

    def specs(rb):
        return [pl.BlockSpec((L, qk_w), lambda b, j: (rb(b, j), 0)),
                pl.BlockSpec((L, qk_w), lambda b, j: (rb(b, j), 1)),
                pl.BlockSpec((L, GROUP_W), lambda b, j: (rb(b, j), 1)),
                pl.BlockSpec((L, 128), lambda b, j: (rb(b, j), Z_SMALL_BLK))]

    n_str = 2 * M_HEADS
    out = jax.ShapeDtypeStruct((geo.rows, GROUP_W), F32)
    return pl.pallas_call(
        _mlstm_kernel,
        out_shape=(out, out),
        grid=(geo.batch, nsteps),
        in_specs=specs(fwd) + specs(bwd) + [
            pl.BlockSpec((1, 128), lambda b, j: (0, 0)),
            pl.BlockSpec((2, L, L), lambda b, j: (0, 0, 0))],
        out_specs=(pl.BlockSpec((L, GROUP_W), lambda b, j: (fwd(b, j), 0)),
                   pl.BlockSpec((L, GROUP_W), lambda b, j: (bwd(b, j), 0))),
        scratch_shapes=[pltpu.VMEM((n_str, M_DQK, M_DV), F32),
                        pltpu.VMEM((n_str, 1, M_DQK), F32),
                        pltpu.VMEM((n_str, 8, 128), F32)],
        compiler_params=_cparams(("parallel", "arbitrary")),
        name="mlstm",
    )(z, z, z, z, z, z, z, z, gate_bias_row, tri)


def _gla_tables(L):
    nlev = int(np.log2(L))
    sel = np.zeros((nlev + 2, L, L), np.float32)
    msk = np.zeros((nlev, L, L), np.float32)
    t = np.arange(L)
    for lev in range(nlev):
        c = 1 << lev
        mid = (t // (2 * c)) * 2 * c + c
        upper = t >= mid
        for r in range(L):
            if upper[r]:
                sel[lev, r, mid[r]:r + 1] = 1.0
            else:
                sel[lev, r, r + 1:mid[r]] = 1.0
        same = (t[:, None] // (2 * c)) == (t[None, :] // (2 * c))
        msk[lev] = (upper[:, None] & ~upper[None, :] & same).astype(np.float32)
    sel[nlev] = (t[None, :] <= t[:, None]).astype(np.float32)
    sel[nlev + 1] = (t[None, :] > t[:, None]).astype(np.float32)
    sel2 = np.stack([sel, sel[:, ::-1, ::-1]]).reshape(2, (nlev + 2) * L, L)
    msk2 = np.stack([msk, msk[:, ::-1, ::-1]])
    return sel2, msk2


def _gla_kernel(qf, kf, vf, sf, qb, kb, vb, sb, w2_ref, ba_ref, sel_ref, msk_ref, of, ob,
                st_ref):
    L = G_CHUNK
    nlev = msk_ref.shape[1]
    hw = G_HEADS * G_DK

    @pl.when(pl.program_id(1) == 0)
    def _():
        st_ref[...] = jnp.zeros_like(st_ref)

    row = lax.broadcasted_iota(jnp.int32, (L, L), 0)
    col = lax.broadcasted_iota(jnp.int32, (L, L), 1)
    eye = row == col
    streams = ((qf, kf, vf, sf, of), (qb, kb, vb, sb, ob))
    for d, (q_ref, k_ref, v_ref, s_ref, o_ref) in enumerate(streams):
        pre = _dot(s_ref[...].astype(BF16), w2_ref[d]) + ba_ref[d]
        la = _log_sigmoid(pre) * (1.0 / G_TAU)
        hi = la.astype(BF16)
        lo = (la - hi.astype(F32)).astype(BF16)
        e2 = _dot(sel_ref[d], jnp.concatenate([hi, lo], axis=1))
        e_all = e2[:, :hw] + e2[:, hw:]
        w_lev = jnp.exp(e_all[:nlev * L])
        run = e_all[nlev * L:(nlev + 1) * L]
        rem = e_all[(nlev + 1) * L:]
        e_run, e_rem = jnp.exp(run), jnp.exp(rem)
        last = L - 1 if d == 0 else 0
        for h in range(G_HEADS):
            sidx = d * G_HEADS + h
            hs = slice(h * G_DK, (h + 1) * G_DK)
            q = q_ref[:, hs] * (G_DK ** -0.5)
            k = k_ref[:, hs]
            vb16 = v_ref[:, h * G_DV:(h + 1) * G_DV].astype(BF16)
            att = jnp.where(eye, jnp.sum(q * k, axis=-1, keepdims=True), 0.0)
            for lev in range(nlev):
                w = w_lev[lev * L:(lev + 1) * L, hs]
                a = _dot_nt((q * w).astype(BF16), (k * w).astype(BF16))
                att = att + a * msk_ref[d, lev]
            st = st_ref[sidx]
            o_ref[:, h * G_DV:(h + 1) * G_DV] = (
                _dot_nt((q * e_run[:, hs]).astype(BF16), st.astype(BF16))
                + _dot(att.astype(BF16), vb16))
            kd = (k * e_rem[:, hs]).astype(BF16)
            st_ref[sidx] = e_run[last:last + 1, hs] * st + _dot_tn(vb16, kd)


def _gla(z, w2pad, b_a, sel, msk, geo):
    L = G_CHUNK
    fwd, bwd, nsteps = geo.scan_blocks(L)
    qk_w = G_HEADS * G_DK
    q_blk = 5120 // qk_w
    v_blk = 6144 // GROUP_W

    def specs(rb):
        return [pl.BlockSpec((L, qk_w), lambda b, j: (rb(b, j), q_blk)),
                pl.BlockSpec((L, qk_w), lambda b, j: (rb(b, j), q_blk + 1)),
                pl.BlockSpec((L, GROUP_W), lambda b, j: (rb(b, j), v_blk)),
                pl.BlockSpec((L, 128), lambda b, j: (rb(b, j), Z_SMALL_BLK))]

    out = jax.ShapeDtypeStruct((geo.rows, GROUP_W), F32)
    return pl.pallas_call(
        _gla_kernel,
        out_shape=(out, out),
        grid=(geo.batch, nsteps),
        in_specs=specs(fwd) + specs(bwd) + [
            pl.BlockSpec(w2pad.shape, lambda b, j: (0, 0, 0)),
            pl.BlockSpec(b_a.shape, lambda b, j: (0, 0, 0)),
            pl.BlockSpec(sel.shape, lambda b, j: (0, 0, 0)),
            pl.BlockSpec(msk.shape, lambda b, j: (0, 0, 0, 0))],
        out_specs=(pl.BlockSpec((L, GROUP_W), lambda b, j: (fwd(b, j), 0)),
                   pl.BlockSpec((L, GROUP_W), lambda b, j: (bwd(b, j), 0))),
        scratch_shapes=[pltpu.VMEM((2 * G_HEADS, G_DV, G_DK), F32)],
        compiler_params=_cparams(("parallel", "arbitrary")),
        name="gla",
    )(z, z, z, z, z, z, z, z, w2pad, b_a, sel, msk)


def _combine_kernel(mf, mb, gf, gb, zo, zs, gm_ref, gg_ref, ym, yg):
    hm = _head_rms(mf[...] + mb[...], gm_ref[...])
    ym[...] = (hm * jax.nn.sigmoid(zo[...])).astype(ym.dtype)
    hg = _head_rms(gf[...] + gb[...], gg_ref[...])
    zg = zs[...]
    yg[...] = (hg * (zg * jax.nn.sigmoid(zg))).astype(yg.dtype)


def _combine(hm_f, hm_b, hg_f, hg_b, z, gm, gg, geo, tr=256):
    blk = lambda c: pl.BlockSpec((tr, GROUP_W), lambda i: (i, c))
    vec = pl.BlockSpec((1, GROUP_W), lambda i: (0, 0))
    out = jax.ShapeDtypeStruct((geo.rows, GROUP_W), BF16)
    return pl.pallas_call(
        _combine_kernel,
        out_shape=(out, out),
        grid=(geo.rows // tr,),
        in_specs=[blk(0), blk(0), blk(0), blk(0), blk(2), blk(7), vec, vec],
        out_specs=(blk(0), blk(0)),
        compiler_params=_cparams(("parallel",)),
        name="combine",
    )(hm_f, hm_b, hg_f, hg_b, z, z, gm.reshape(1, -1), gg.reshape(1, -1))


def _sgu_kernel(u_ref, v_ref, ln_ref, w_ref, b_ref, gs_ref, o_ref):
    u = jax.nn.gelu(u_ref[...])
    v = jax.nn.gelu(v_ref[...])
    gw = GROUP_W // S_GROUPS
    outs = []
    for g in range(S_GROUPS):
        vg = v[:, g * gw:(g + 1) * gw]
        parts = []
        for h in range(gw // NORM_HEAD):
            vh = vg[:, h * NORM_HEAD:(h + 1) * NORM_HEAD]
            vh = vh - jnp.mean(vh, axis=-1, keepdims=True)
            parts.append(vh * lax.rsqrt(jnp.mean(vh * vh, axis=-1, keepdims=True) + EPS))
        vn = (parts[0] if len(parts) == 1 else jnp.concatenate(parts, axis=-1))
        vn = vn * ln_ref[:, g * gw:(g + 1) * gw]
        s = _dot(w_ref[g], vn.astype(BF16)) + b_ref[:, g:g + 1]
        outs.append(u[:, g * gw:(g + 1) * gw] * s)
    o_ref[...] = _head_rms(jnp.concatenate(outs, axis=-1), gs_ref[...]).astype(o_ref.dtype)


def _sgu(z, ln_g, w_s16, b_s_t, gs, geo):
    L = S_CHUNK
    return pl.pallas_call(
        _sgu_kernel,
        out_shape=jax.ShapeDtypeStruct((geo.rows, GROUP_W), BF16),
        grid=(geo.rows // L,),
        in_specs=[pl.BlockSpec((L, GROUP_W), lambda i: (i, 3)),
                  pl.BlockSpec((L, GROUP_W), lambda i: (i, 4)),
                  pl.BlockSpec((1, GROUP_W), lambda i: (0, 0)),
                  pl.BlockSpec((S_GROUPS, L, L), lambda i: (0, 0, 0)),
                  pl.BlockSpec((L, S_GROUPS), lambda i: (0, 0)),
                  pl.BlockSpec((1, GROUP_W), lambda i: (0, 0))],
        out_specs=pl.BlockSpec((L, GROUP_W), lambda i: (i, 0)),
        compiler_params=_cparams(("parallel",)),
        name="sgu",
    )(z, z, ln_g.reshape(1, -1), w_s16, b_s_t, gs.reshape(1, -1))


def _conv_kernel(h_ref, bg_ref, cg_ref, w_ref, gc_ref, o_ref, *, tr, n_ctx_tiles, ctx_period):
    y = cg_ref[...] * h_ref[...]
    period = jnp.where(pl.program_id(0) < n_ctx_tiles, ctx_period, GRID_W)
    pos = lax.broadcasted_iota(jnp.int32, (tr, 1), 0) & (period - 1)
    prev = jnp.where(pos != 0, pltpu.roll(y, 1, axis=0), 0.0)
    nxt = jnp.where(pos != period - 1, pltpu.roll(y, tr - 1, axis=0), 0.0)
    c = w_ref[0:1, :] * prev + w_ref[1:2, :] * y + w_ref[2:3, :] * nxt
    o_ref[...] = _head_rms(bg_ref[...] * c, gc_ref[...]).astype(o_ref.dtype)


def _conv(z, conv_w, gc, geo):
    tr = min(geo.ctx_len, 256)
    assert geo.ctx_len % tr == 0 and tr % GRID_W == 0 and (tr & (tr - 1)) == 0
    blk = lambda c: pl.BlockSpec((tr, GROUP_W), lambda i: (i, c))
    return pl.pallas_call(
        functools.partial(_conv_kernel, tr=tr, n_ctx_tiles=geo.ctx_rows // tr, ctx_period=tr),
        out_shape=jax.ShapeDtypeStruct((geo.rows, GROUP_W), BF16),
        grid=(geo.rows // tr,),
        in_specs=[blk(8), blk(9), blk(10),
                  pl.BlockSpec((CONV_W, GROUP_W), lambda i: (0, 0)),
                  pl.BlockSpec((1, GROUP_W), lambda i: (0, 0))],
        out_specs=blk(0),
        compiler_params=_cparams(("parallel",)),
        name="short_conv",
    )(z, z, z, conv_w, gc.reshape(1, -1))


def _regroup_w_in(w_in):
    w16 = w_in.astype(BF16)
    segs = [w16[..., _SPLIT_OFF[i]:_SPLIT_OFF[i + 1]] for i in _SEG_ORDER]
    pad = jnp.zeros(w_in.shape[:-1] + (Z_COLS - sum(_SPLIT_SIZES),), BF16)
    return jnp.concatenate(segs + [pad], axis=-1)


def kernel(x, c, ctx, c_ctx, norm1_g, norm2_g, mod_a, mod_b, mod_bias, w_in, mlstm_gate_bias,
           gla_w_a2, gla_b_a, sgu_ln_g, sgu_w, sgu_b, conv_w, mix_norm_g, w_out, w_ff1, w_ff2,
           final_norm_g):
    batch, seq, d = x.shape
    ctx_len = ctx.shape[1]
    depth = mod_a.shape[0]
    assert ctx_len % max(M_CHUNK, G_CHUNK, S_CHUNK) == 0 and seq % 256 == 0 and d == N_MIXERS * GROUP_W
    geo = _Geo(batch, ctx_len, seq)

    cvec = jnp.zeros((8, d), F32).at[0].set(c_ctx).at[1:1 + batch].set(c)
    mods = _modulation_all(cvec, mod_a, mod_b, mod_bias)
    mods = mods.reshape(depth, 8, N_MOD, d)[:, :1 + batch]
    mods = jnp.pad(mods, ((0, 0), (0, 0), (0, 8 - N_MOD), (0, 0)))

    tri = jnp.asarray(np.stack([np.tril(np.ones((M_CHUNK, M_CHUNK), np.float32)),
                                np.triu(np.ones((M_CHUNK, M_CHUNK), np.float32))]), BF16)
    sel_np, msk_np = _gla_tables(G_CHUNK)
    sel = jnp.asarray(sel_np, BF16)
    msk = jnp.asarray(msk_np, F32)

    w_in16 = _regroup_w_in(w_in)
    xs = jnp.concatenate([ctx.reshape(batch * ctx_len, d), x.reshape(batch * seq, d)], axis=0)
    tm = max(t for t in (768, 512, 256) if geo.rows % t == 0)
    tm_big = max(t for t in (1536, 768, 512, 256) if geo.rows % t == 0)
    sub0 = np.arange(geo.rows // GATE_SUB) * GATE_SUB
    sub_group = np.where(sub0 < geo.ctx_rows, 0, 1 + (sub0 - geo.ctx_rows) // seq)

    for l in range(depth):
        modt = mods[l]
        gate1_tab = modt[sub_group, 2][:, None, :]
        gate2_tab = modt[sub_group, 5][:, None, :]
        gate_row = jnp.zeros((1, 128), F32).at[0, GATE_COL0:GATE_COL0 + 4 * M_HEADS].set(
            mlstm_gate_bias[l])
        w2pad = jnp.zeros((2, 128, G_HEADS * G_DK), F32)
        for dd in range(2):
            w2pad = w2pad.at[dd, ALR_COL0 + dd * G_RANK:ALR_COL0 + (dd + 1) * G_RANK].set(
                gla_w_a2[l, dd])
        gm, gs, gg, gc = (mix_norm_g[l, i * GROUP_W:(i + 1) * GROUP_W] for i in range(N_MIXERS))

        h1 = _modulate(xs, norm1_g[l], modt, geo, shift_row=0, scale_row=1)
        z = _matmul([h1], w_in16[l], out_dtype=F32, tm=tm, tn=1280, name="mm_in")
        hm_f, hm_b = _mlstm(z, gate_row, tri, geo)
        hg_f, hg_b = _gla(z, w2pad.astype(BF16), gla_b_a[l].reshape(2, 1, -1), sel, msk, geo)
        y_m, y_g = _combine(hm_f, hm_b, hg_f, hg_b, z, gm, gg, geo)
        y_s = _sgu(z, sgu_ln_g[l], sgu_w[l].astype(BF16), sgu_b[l].T, gs, geo)
        y_c = _conv(z, conv_w[l], gc, geo)
        xs = _matmul([y_m, y_s, y_g, y_c], w_out[l].astype(BF16), out_dtype=F32, epi="resid",
                     x=xs, gate_tab=gate1_tab, tm=tm_big, tn=512, name="mm_out")
        h2 = _modulate(xs, norm2_g[l], modt, geo, shift_row=3, scale_row=4)
        hid = _matmul([h2], w_ff1[l].astype(BF16), out_dtype=BF16, epi="relu2",
                      tm=tm, tn=1024, name="mm_ff1")
        xs = _matmul([hid], w_ff2[l].astype(BF16), out_dtype=F32, epi="resid", x=xs,
                     gate_tab=gate2_tab, tm=tm_big, tn=1024, tk=2048, name="mm_ff2")

    out = _final_norm(xs, final_norm_g, geo)
    return out.reshape(batch, seq, d)
```

```python
import functools

import numpy as np
import jax
import jax.numpy as jnp
from jax import lax
from jax.experimental import pallas as pl
from jax.experimental.pallas import tpu as pltpu

F32 = jnp.float32
BF16 = jnp.bfloat16

GRID_W = 64
N_MOD = 6
N_MIXERS = 4
GROUP_W = 1024
NORM_HEAD = 256
EPS = 1e-6
NEG_INIT = -1e30

M_HEADS = 4
M_DV = GROUP_W // M_HEADS
M_DQK = M_DV // 2
M_CHUNK = 128
GATE_CAP = 15.0
S_GROUPS = 4
S_CHUNK = 128
G_HEADS = 4
G_DV = GROUP_W // G_HEADS
G_DK = G_DV // 2
G_RANK = 16
G_TAU = 16.0
G_CHUNK = 128
CONV_W = 3

_SPLIT_SIZES = (M_HEADS * M_DQK, M_HEADS * M_DQK, GROUP_W, GROUP_W, 2 * 2 * M_HEADS,
                GROUP_W, GROUP_W,
                G_HEADS * G_DK, G_HEADS * G_DK, GROUP_W, GROUP_W, 2 * G_RANK,
                GROUP_W, GROUP_W, GROUP_W)
_SPLIT_OFF = tuple(int(i) for i in np.concatenate([[0], np.cumsum(_SPLIT_SIZES)]))
_SEG_ORDER = (0, 1, 2, 3, 5, 6, 7, 8, 9, 10, 12, 13, 14, 4, 11)
Z_WIDE = 11264
Z_COLS = 11520
Z_SMALL_BLK = Z_WIDE // 128
GATE_COL0 = 0
ALR_COL0 = 2 * 2 * M_HEADS

VMEM_LIMIT = 56 * 1024 * 1024


def _cparams(sem, vmem=VMEM_LIMIT):
    return pltpu.CompilerParams(dimension_semantics=sem, vmem_limit_bytes=vmem)


def _log_sigmoid(x):
    return jnp.minimum(x, 0.0) - jnp.log1p(jnp.exp(-jnp.abs(x)))


def _split3(x):
    hi = x.astype(BF16)
    r1 = x - hi.astype(F32)
    mid = r1.astype(BF16)
    lo = (r1 - mid.astype(F32)).astype(BF16)
    return hi, mid, lo


def _dot(a, b):
    return jnp.dot(a, b, preferred_element_type=F32)


def _dot_nt(a, b):
    return lax.dot_general(a, b, (((1,), (1,)), ((), ())), preferred_element_type=F32)


def _dot_tn(a, b):
    return lax.dot_general(a, b, (((0,), (0,)), ((), ())), preferred_element_type=F32)


def _head_rms(y, g):
    outs = []
    for h in range(y.shape[-1] // NORM_HEAD):
        yh = y[:, h * NORM_HEAD:(h + 1) * NORM_HEAD]
        ms = jnp.mean(yh * yh, axis=-1, keepdims=True)
        outs.append(yh * lax.rsqrt(ms + EPS))
    return jnp.concatenate(outs, axis=-1) * g


def _mod_kernel(c_ref, a_ref, b_ref, bias_ref, o_ref, t_ref):
    @pl.when(pl.program_id(1) == 0)
    def _():
        cv = c_ref[...]
        s = cv * jax.nn.sigmoid(cv)
        t_ref[...] = _dot(s.astype(BF16), a_ref[0].astype(BF16))

    o_ref[0] = _dot(t_ref[...].astype(BF16), b_ref[0].astype(BF16)) + bias_ref[0]


def _modulation_all(cvec, mod_a, mod_b, mod_bias, tn=2048):
    depth, d, rank = mod_a.shape
    n = mod_b.shape[-1]
    return pl.pallas_call(
        _mod_kernel,
        out_shape=jax.ShapeDtypeStruct((depth, 8, n), F32),
        grid=(depth, n // tn),
        in_specs=[pl.BlockSpec((8, d), lambda l, j: (0, 0)),
                  pl.BlockSpec((1, d, rank), lambda l, j: (l, 0, 0)),
                  pl.BlockSpec((1, rank, tn), lambda l, j: (l, 0, j)),
                  pl.BlockSpec((1, 1, tn), lambda l, j: (l, 0, j))],
        out_specs=pl.BlockSpec((1, 8, tn), lambda l, j: (l, 0, j)),
        scratch_shapes=[pltpu.VMEM((8, rank), F32)],
        compiler_params=_cparams(("arbitrary", "arbitrary")),
        name="modulation",
    )(cvec, mod_a, mod_b, mod_bias.reshape(depth, 1, n))


def _modulate_kernel(x_ref, g_ref, m_ref, o_ref, *, shift_row, scale_row):
    x = x_ref[...]
    ms = jnp.mean(x * x, axis=-1, keepdims=True)
    xn = x * lax.rsqrt(ms + EPS) * g_ref[...]
    sc = m_ref[0, scale_row:scale_row + 1, :]
    sh = m_ref[0, shift_row:shift_row + 1, :]
    o_ref[...] = (xn * (1.0 + sc) + sh).astype(o_ref.dtype)


def _modulate(x, g, modt, geo, *, shift_row, scale_row, tr=256):
    rows, d = x.shape
    return pl.pallas_call(
        functools.partial(_modulate_kernel, shift_row=shift_row, scale_row=scale_row),
        out_shape=jax.ShapeDtypeStruct((rows, d), BF16),
        grid=(rows // tr,),
        in_specs=[pl.BlockSpec((tr, d), lambda i: (i, 0)),
                  pl.BlockSpec((1, d), lambda i: (0, 0)),
                  pl.BlockSpec((1, 8, d), lambda i: (geo.group(i * tr), 0, 0))],
        out_specs=pl.BlockSpec((tr, d), lambda i: (i, 0)),
        compiler_params=_cparams(("parallel",)),
        name="modulate",
    )(x, g.reshape(1, d), modt)


def _final_norm_kernel(x_ref, g_ref, o_ref):
    x = x_ref[...]
    ms = jnp.mean(x * x, axis=-1, keepdims=True)
    o_ref[...] = x * lax.rsqrt(ms + EPS) * g_ref[...]


def _final_norm(x, g, geo, tr=256):
    rows, d = x.shape
    skip = geo.ctx_rows // tr
    return pl.pallas_call(
        _final_norm_kernel,
        out_shape=jax.ShapeDtypeStruct((rows - geo.ctx_rows, d), F32),
        grid=((rows - geo.ctx_rows) // tr,),
        in_specs=[pl.BlockSpec((tr, d), lambda i: (i + skip, 0)),
                  pl.BlockSpec((1, d), lambda i: (0, 0))],
        out_specs=pl.BlockSpec((tr, d), lambda i: (i, 0)),
        compiler_params=_cparams(("parallel",)),
        name="final_norm",
    )(x, g.reshape(1, d))


def _mm_kernel(*refs, n_a, epi, nk, n_sub, trans_b):
    a_refs = refs[:n_a]
    b_ref = refs[n_a]
    pos = n_a + 1
    if epi == "resid":
        x_ref, g_ref = refs[pos], refs[pos + 1]
        pos += 2
    o_ref = refs[pos]

    part = None
    off = 0
    for a in a_refs:
        kk = a.shape[1]
        if trans_b:
            p = _dot_nt(a[...], b_ref[:, off:off + kk])
        else:
            p = _dot(a[...], b_ref[off:off + kk, :])
        part = p if part is None else part + p
        off += kk

    if epi == "relu2":
        r = jnp.maximum(part, 0.0)
        o_ref[...] = (r * r).astype(o_ref.dtype)
    elif epi == "resid":
        sub = part.shape[0] // n_sub
        gated = jnp.concatenate(
            [g_ref[r] * part[r * sub:(r + 1) * sub] for r in range(n_sub)], axis=0)
        if nk == 1:
            o_ref[...] = x_ref[...] + gated
        else:
            o_ref[...] = jnp.where(pl.program_id(2) == 0, x_ref[...], o_ref[...]) + gated
    else:
        o_ref[...] = part.astype(o_ref.dtype)


GATE_SUB = 256


def _matmul(a_list, b, layer, *, out_dtype, epi="none", x=None, gate_tab=None,
            tm=768, tn=1024, tk=None, trans_b=False, name="matmul"):
    rows = a_list[0].shape[0]
    n, ktot = b.shape[1:] if trans_b else b.shape[:0:-1]
    widths = [a.shape[1] for a in a_list]
    assert sum(widths) == ktot and rows % tm == 0 and n % tn == 0 and tm % GATE_SUB == 0
    if tk is None:
        tk = ktot
    nk = ktot // tk
    assert ktot % tk == 0 and (nk == 1 or (len(a_list) == 1 and epi == "resid"))
    n_sub = tm // GATE_SUB
    in_specs = []
    for w in widths:
        wk = w if nk == 1 else tk
        in_specs.append(pl.BlockSpec((tm, wk), lambda i, j, k: (i, k)))
    if trans_b:
        in_specs.append(pl.BlockSpec((None, tn, tk), lambda i, j, k: (layer, j, k)))
    else:
        in_specs.append(pl.BlockSpec((None, tk, tn), lambda i, j, k: (layer, k, j)))
    args = list(a_list) + [b]
    if epi == "resid":
        in_specs.append(pl.BlockSpec((tm, tn), lambda i, j, k: (i, j)))
        in_specs.append(pl.BlockSpec((n_sub, 1, tn), lambda i, j, k: (i, 0, j)))
        args += [x, gate_tab]
    return pl.pallas_call(
        functools.partial(_mm_kernel, n_a=len(a_list), epi=epi, nk=nk, n_sub=n_sub,
                          trans_b=trans_b),
        out_shape=jax.ShapeDtypeStruct((rows, n), out_dtype),
        grid=(rows // tm, n // tn, nk),
        in_specs=in_specs,
        out_specs=pl.BlockSpec((tm, tn), lambda i, j, k: (i, j)),
        compiler_params=_cparams(("parallel", "parallel", "arbitrary")),
        name=name,
    )(*args)


class _Geo:
    def __init__(self, batch, ctx_len, seq):
        self.batch, self.ctx_len, self.seq = batch, ctx_len, seq
        self.ctx_rows = batch * ctx_len
        self.rows = self.ctx_rows + batch * seq

    def group(self, row0):
        return jnp.where(row0 < self.ctx_rows, 0, 1 + (row0 - self.ctx_rows) // self.seq)

    def scan_blocks(self, chunk):
        nc, nl = self.ctx_len // chunk, self.seq // chunk
        base = self.ctx_rows // chunk

        def fwd(b, j):
            return jnp.where(j < nc, b * nc + j, base + b * nl + (j - nc))

        def bwd(b, j):
            return jnp.where(j < nc, b * nc + (nc - 1 - j), base + b * nl + (nl - 1 - (j - nc)))

        return fwd, bwd, nc + nl


def _mlstm_stages(qf, kf, vf, sf, qb, kb, vb, sb, bias_ref, tri_ref, of, ob,
                  c_ref, n_ref, m_ref):
    L = M_CHUNK
    row = lax.broadcasted_iota(jnp.int32, (L, L), 0)
    col = lax.broadcasted_iota(jnp.int32, (L, L), 1)
    streams = ((qf, kf, vf, sf, of), (qb, kb, vb, sb, ob))
    work = []
    for d, (q_ref, k_ref, v_ref, s_ref, o_ref) in enumerate(streams):
        mask = (row >= col) if d == 0 else (row <= col)
        g = s_ref[...] + bias_ref[...]
        g = GATE_CAP * jnp.tanh(g / GATE_CAP)
        lf = _log_sigmoid(g)
        hi, mid, lo = _split3(lf)
        cs = _dot(tri_ref[d], jnp.concatenate([hi, mid, lo], axis=1))
        bcum = cs[:, 0:128] + cs[:, 128:256] + cs[:, 256:384]
        bcum_t = bcum.T
        g_t = g.T
        last = L - 1 if d == 0 else 0
        for h in range(M_HEADS):
            w = dict(sidx=d * M_HEADS + h, o_ref=o_ref, h=h, mask=mask)
            ci = GATE_COL0 + d * 2 * M_HEADS + h
            cf = ci + M_HEADS
            w["b_col"], w["b_row"] = bcum[:, cf:cf + 1], bcum_t[cf:cf + 1, :]
            w["i_col"], w["i_row"] = g[:, ci:ci + 1], g_t[ci:ci + 1, :]
            w["b_tot"] = w["b_col"][last:last + 1, :]
            w["q"] = q_ref[:, h * M_DQK:(h + 1) * M_DQK] * (M_DQK ** -0.5)
            w["k"] = k_ref[:, h * M_DQK:(h + 1) * M_DQK]
            w["vb16"] = v_ref[:, h * M_DV:(h + 1) * M_DV].astype(BF16)
            work.append(w)
        yield

    for w in work:
        sidx = w["sidx"]
        w["c_st"], w["n_st"], w["m_st"] = c_ref[sidx], n_ref[sidx], m_ref[sidx][0:1, 0:1]
        w["qb16"], w["kb16"] = w["q"].astype(BF16), w["k"].astype(BF16)
        a_inter = w["b_col"] + w["m_st"]
        dmat = jnp.where(w["mask"], w["b_col"] - w["b_row"] + w["i_row"], -jnp.inf)
        w["m_t"] = jnp.maximum(a_inter, jnp.max(dmat, axis=-1, keepdims=True))
        w["w_inter"] = jnp.exp(a_inter - w["m_t"])
        w["p"] = jnp.exp(dmat - w["m_t"])
        w["qk"] = _dot_nt(w["qb16"], w["kb16"])
        w["qc"] = _dot(w["qb16"], w["c_st"].astype(BF16))
        w["qn"] = jnp.sum(w["q"] * w["n_st"], axis=-1, keepdims=True)
        yield

    for w in work:
        h = w["h"]
        s = w["qk"] * w["p"]
        num = w["w_inter"] * w["qc"] + _dot(s.astype(BF16), w["vb16"])
        den = w["w_inter"] * w["qn"] + jnp.sum(s, axis=-1, keepdims=True)
        w["o_ref"][:, h * M_DV:(h + 1) * M_DV] = (
            num / jnp.maximum(jnp.abs(den), jnp.exp(-w["m_t"])))
        yield

    for w in work:
        sidx = w["sidx"]
        g_end = w["b_tot"] - w["b_col"] + w["i_col"]
        m_new = jnp.maximum(w["b_tot"] + w["m_st"], jnp.max(g_end, axis=0, keepdims=True))
        w_old = jnp.exp(w["b_tot"] + w["m_st"] - m_new)
        kw = w["k"] * jnp.exp(g_end - m_new)
        c_ref[sidx] = w_old * w["c_st"] + _dot_tn(kw.astype(BF16), w["vb16"])
        n_ref[sidx] = w_old * w["n_st"] + jnp.sum(kw, axis=0, keepdims=True)
        m_ref[sidx] = jnp.broadcast_to(m_new, m_ref.shape[1:])
        yield


def _gla_tables(L):
    nlev = int(np.log2(L))
    sel = np.zeros((nlev + 2, L, L), np.float32)
    msk = np.zeros((nlev, L, L), np.float32)
    t = np.arange(L)
    for lev in range(nlev):
        c = 1 << lev
        mid = (t // (2 * c)) * 2 * c + c
        upper = t >= mid
        for r in range(L):
            if upper[r]:
                sel[lev, r, mid[r]:r + 1] = 1.0
            else:
                sel[lev, r, r + 1:mid[r]] = 1.0
        same = (t[:, None] // (2 * c)) == (t[None, :] // (2 * c))
        msk[lev] = (upper[:, None] & ~upper[None, :] & same).astype(np.float32)
    sel[nlev] = (t[None, :] <= t[:, None]).astype(np.float32)
    sel[nlev + 1] = (t[None, :] > t[:, None]).astype(np.float32)
    sel2 = np.stack([sel, sel[:, ::-1, ::-1]]).reshape(2, (nlev + 2) * L, L)
    msk2 = np.stack([msk, msk[:, ::-1, ::-1]])
    return sel2, msk2


def _gla_stages(qf, kf, vf, sf, qb, kb, vb, sb, w2_ref, ba_ref, sel_ref, msk_ref, of, ob,
                st_ref):
    L = G_CHUNK
    nlev = msk_ref.shape[1]
    hw = G_HEADS * G_DK
    row = lax.broadcasted_iota(jnp.int32, (L, L), 0)
    col = lax.broadcasted_iota(jnp.int32, (L, L), 1)
    eye = row == col
    streams = ((qf, kf, vf, sf, of), (qb, kb, vb, sb, ob))
    work = []
    for d, (q_ref, k_ref, v_ref, s_ref, o_ref) in enumerate(streams):
        pre = _dot(s_ref[...].astype(BF16), w2_ref[d]) + ba_ref[d]
        la = _log_sigmoid(pre) * (1.0 / G_TAU)
        hi = la.astype(BF16)
        lo = (la - hi.astype(F32)).astype(BF16)
        e2 = _dot(sel_ref[d], jnp.concatenate([hi, lo], axis=1))
        e_all = e2[:, :hw] + e2[:, hw:]
        w_lev = jnp.exp(e_all[:nlev * L])
        e_run = jnp.exp(e_all[nlev * L:(nlev + 1) * L])
        e_rem = jnp.exp(e_all[(nlev + 1) * L:])
        last = L - 1 if d == 0 else 0
        for h in range(G_HEADS):
            hs = slice(h * G_DK, (h + 1) * G_DK)
            work.append(dict(
                d=d, h=h, sidx=d * G_HEADS + h, o_ref=o_ref,
                q=q_ref[:, hs] * (G_DK ** -0.5), k=k_ref[:, hs],
                vb16=v_ref[:, h * G_DV:(h + 1) * G_DV].astype(BF16),
                w_lev=[w_lev[lev * L:(lev + 1) * L, hs] for lev in range(nlev)],
                e_run=e_run[:, hs], e_rem=e_rem[:, hs], e_last=e_run[last:last + 1, hs]))
        yield

    for w in work:
        q, k = w["q"], w["k"]
        att = jnp.where(eye, jnp.sum(q * k, axis=-1, keepdims=True), 0.0)
        for lev in range(nlev):
            wl = w["w_lev"][lev]
            a = _dot_nt((q * wl).astype(BF16), (k * wl).astype(BF16))
            att = att + a * msk_ref[w["d"], lev]
        w["att"] = att
        yield

    for w in work:
        h = w["h"]
        w["st"] = st_ref[w["sidx"]]
        w["o_ref"][:, h * G_DV:(h + 1) * G_DV] = (
            _dot_nt((w["q"] * w["e_run"]).astype(BF16), w["st"].astype(BF16))
            + _dot(w["att"].astype(BF16), w["vb16"]))
        yield

    for w in work:
        kd = (w["k"] * w["e_rem"]).astype(BF16)
        st_ref[w["sidx"]] = w["e_last"] * w["st"] + _dot_tn(w["vb16"], kd)
        yield


_DONE = object()


def _scans_kernel(mqf, mkf, mvf, gqf, gkf, gvf, sf, mqb, mkb, mvb, gqb, gkb, gvb, sb,
                  bias_ref, tri_ref, w2_ref, ba_ref, sel_ref, msk_ref,
                  mof, mob, gof, gob, c_ref, n_ref, m_ref, st_ref):
    @pl.when(pl.program_id(1) == 0)
    def _():
        c_ref[...] = jnp.zeros_like(c_ref)
        n_ref[...] = jnp.zeros_like(n_ref)
        m_ref[...] = jnp.full_like(m_ref, NEG_INIT)
        st_ref[...] = jnp.zeros_like(st_ref)

    gens = [_mlstm_stages(mqf, mkf, mvf, sf, mqb, mkb, mvb, sb, bias_ref, tri_ref, mof, mob,
                          c_ref, n_ref, m_ref),
            _gla_stages(gqf, gkf, gvf, sf, gqb, gkb, gvb, sb, w2_ref, ba_ref, sel_ref, msk_ref,
                        gof, gob, st_ref)]
    while gens:
        gens = [g for g in gens if next(g, _DONE) is not _DONE]


def _scan_mixers(z, gate_bias_row, tri, w2pad, b_a, sel, msk, geo):
    assert M_CHUNK == G_CHUNK
    L = M_CHUNK
    fwd, bwd, nsteps = geo.scan_blocks(L)
    qk_w = M_HEADS * M_DQK
    gq_blk = 5120 // qk_w
    gv_blk = 6144 // GROUP_W

    def specs(rb):
        blk = lambda w, c: pl.BlockSpec((L, w), lambda b, j: (rb(b, j), c))
        return [blk(qk_w, 0), blk(qk_w, 1), blk(GROUP_W, 1),
                blk(qk_w, gq_blk), blk(qk_w, gq_blk + 1), blk(GROUP_W, gv_blk),
                blk(128, Z_SMALL_BLK)]

    full = lambda a: pl.BlockSpec(a.shape, lambda b, j: (0,) * a.ndim)
    consts = [gate_bias_row, tri, w2pad, b_a, sel, msk]
    out = jax.ShapeDtypeStruct((geo.rows, GROUP_W), F32)
    o_f = pl.BlockSpec((L, GROUP_W), lambda b, j: (fwd(b, j), 0))
    o_b = pl.BlockSpec((L, GROUP_W), lambda b, j: (bwd(b, j), 0))
    n_str = 2 * M_HEADS
    return pl.pallas_call(
        _scans_kernel,
        out_shape=(out, out, out, out),
        grid=(geo.batch, nsteps),
        in_specs=specs(fwd) + specs(bwd) + [full(a) for a in consts],
        out_specs=(o_f, o_b, o_f, o_b),
        scratch_shapes=[pltpu.VMEM((n_str, M_DQK, M_DV), F32),
                        pltpu.VMEM((n_str, 1, M_DQK), F32),
                        pltpu.VMEM((n_str, 8, 128), F32),
                        pltpu.VMEM((2 * G_HEADS, G_DV, G_DK), F32)],
        compiler_params=_cparams(("parallel", "arbitrary")),
        name="scan_mixers",
    )(*([z] * 14), *consts)


def _combine_kernel(mf, mb, gf, gb, zo, zs, gm_ref, gg_ref, ym, yg):
    hm = _head_rms(mf[...] + mb[...], gm_ref[...])
    ym[...] = (hm * jax.nn.sigmoid(zo[...])).astype(ym.dtype)
    hg = _head_rms(gf[...] + gb[...], gg_ref[...])
    zg = zs[...]
    yg[...] = (hg * (zg * jax.nn.sigmoid(zg))).astype(yg.dtype)


def _combine(hm_f, hm_b, hg_f, hg_b, z, gm, gg, geo, tr=256):
    blk = lambda c: pl.BlockSpec((tr, GROUP_W), lambda i: (i, c))
    vec = pl.BlockSpec((1, GROUP_W), lambda i: (0, 0))
    out = jax.ShapeDtypeStruct((geo.rows, GROUP_W), BF16)
    return pl.pallas_call(
        _combine_kernel,
        out_shape=(out, out),
        grid=(geo.rows // tr,),
        in_specs=[blk(0), blk(0), blk(0), blk(0), blk(2), blk(7), vec, vec],
        out_specs=(blk(0), blk(0)),
        compiler_params=_cparams(("parallel",)),
        name="combine",
    )(hm_f, hm_b, hg_f, hg_b, z, z, gm.reshape(1, -1), gg.reshape(1, -1))


def _sgu_kernel(u_ref, v_ref, ln_ref, w_ref, b_ref, gs_ref, o_ref):
    u = jax.nn.gelu(u_ref[...])
    v = jax.nn.gelu(v_ref[...])
    gw = GROUP_W // S_GROUPS
    outs = []
    for g in range(S_GROUPS):
        vg = v[:, g * gw:(g + 1) * gw]
        parts = []
        for h in range(gw // NORM_HEAD):
            vh = vg[:, h * NORM_HEAD:(h + 1) * NORM_HEAD]
            vh = vh - jnp.mean(vh, axis=-1, keepdims=True)
            parts.append(vh * lax.rsqrt(jnp.mean(vh * vh, axis=-1, keepdims=True) + EPS))
        vn = (parts[0] if len(parts) == 1 else jnp.concatenate(parts, axis=-1))
        vn = vn * ln_ref[:, g * gw:(g + 1) * gw]
        s = _dot(w_ref[g], vn.astype(BF16)) + b_ref[:, g:g + 1]
        outs.append(u[:, g * gw:(g + 1) * gw] * s)
    o_ref[...] = _head_rms(jnp.concatenate(outs, axis=-1), gs_ref[...]).astype(o_ref.dtype)


def _sgu(z, ln_g, w_s16, b_s_t, gs, geo):
    L = S_CHUNK
    return pl.pallas_call(
        _sgu_kernel,
        out_shape=jax.ShapeDtypeStruct((geo.rows, GROUP_W), BF16),
        grid=(geo.rows // L,),
        in_specs=[pl.BlockSpec((L, GROUP_W), lambda i: (i, 3)),
                  pl.BlockSpec((L, GROUP_W), lambda i: (i, 4)),
                  pl.BlockSpec((1, GROUP_W), lambda i: (0, 0)),
                  pl.BlockSpec((S_GROUPS, L, L), lambda i: (0, 0, 0)),
                  pl.BlockSpec((L, S_GROUPS), lambda i: (0, 0)),
                  pl.BlockSpec((1, GROUP_W), lambda i: (0, 0))],
        out_specs=pl.BlockSpec((L, GROUP_W), lambda i: (i, 0)),
        compiler_params=_cparams(("parallel",)),
        name="sgu",
    )(z, z, ln_g.reshape(1, -1), w_s16, b_s_t, gs.reshape(1, -1))


def _conv_kernel(h_ref, bg_ref, cg_ref, w_ref, gc_ref, o_ref, *, tr, n_ctx_tiles, ctx_period):
    y = cg_ref[...] * h_ref[...]
    period = jnp.where(pl.program_id(0) < n_ctx_tiles, ctx_period, GRID_W)
    pos = lax.broadcasted_iota(jnp.int32, (tr, 1), 0) & (period - 1)
    prev = jnp.where(pos != 0, pltpu.roll(y, 1, axis=0), 0.0)
    nxt = jnp.where(pos != period - 1, pltpu.roll(y, tr - 1, axis=0), 0.0)
    c = w_ref[0:1, :] * prev + w_ref[1:2, :] * y + w_ref[2:3, :] * nxt
    o_ref[...] = _head_rms(bg_ref[...] * c, gc_ref[...]).astype(o_ref.dtype)


def _conv(z, conv_w, gc, geo):
    tr = min(geo.ctx_len, 256)
    assert geo.ctx_len % tr == 0 and tr % GRID_W == 0 and (tr & (tr - 1)) == 0
    blk = lambda c: pl.BlockSpec((tr, GROUP_W), lambda i: (i, c))
    return pl.pallas_call(
        functools.partial(_conv_kernel, tr=tr, n_ctx_tiles=geo.ctx_rows // tr, ctx_period=tr),
        out_shape=jax.ShapeDtypeStruct((geo.rows, GROUP_W), BF16),
        grid=(geo.rows // tr,),
        in_specs=[blk(8), blk(9), blk(10),
                  pl.BlockSpec((CONV_W, GROUP_W), lambda i: (0, 0)),
                  pl.BlockSpec((1, GROUP_W), lambda i: (0, 0))],
        out_specs=blk(0),
        compiler_params=_cparams(("parallel",)),
        name="short_conv",
    )(z, z, z, conv_w, gc.reshape(1, -1))


def _regroup_w_in_t(w_in):
    w_t = jnp.swapaxes(w_in, 1, 2)
    segs = [w_t[:, _SPLIT_OFF[i]:_SPLIT_OFF[i + 1]] for i in _SEG_ORDER]
    pad = jnp.zeros((w_in.shape[0], Z_COLS - sum(_SPLIT_SIZES), w_in.shape[1]), w_in.dtype)
    return jnp.concatenate(segs + [pad], axis=1).astype(BF16)


def kernel(x, c, ctx, c_ctx, norm1_g, norm2_g, mod_a, mod_b, mod_bias, w_in, mlstm_gate_bias,
           gla_w_a2, gla_b_a, sgu_ln_g, sgu_w, sgu_b, conv_w, mix_norm_g, w_out, w_ff1, w_ff2,
           final_norm_g):
    batch, seq, d = x.shape
    ctx_len = ctx.shape[1]
    depth = mod_a.shape[0]
    assert ctx_len % max(M_CHUNK, G_CHUNK, S_CHUNK) == 0 and seq % 256 == 0 and d == N_MIXERS * GROUP_W
    geo = _Geo(batch, ctx_len, seq)

    cvec = jnp.zeros((8, d), F32).at[0].set(c_ctx).at[1:1 + batch].set(c)
    mods = _modulation_all(cvec, mod_a, mod_b, mod_bias)
    mods = mods.reshape(depth, 8, N_MOD, d)[:, :1 + batch]
    mods = jnp.pad(mods, ((0, 0), (0, 0), (0, 8 - N_MOD), (0, 0)))

    tri = jnp.asarray(np.stack([np.tril(np.ones((M_CHUNK, M_CHUNK), np.float32)),
                                np.triu(np.ones((M_CHUNK, M_CHUNK), np.float32))]), BF16)
    sel_np, msk_np = _gla_tables(G_CHUNK)
    sel = jnp.asarray(sel_np, BF16)
    msk = jnp.asarray(msk_np, F32)

    w_in16 = _regroup_w_in_t(w_in)
    w_out16, w_ff1_16, w_ff2_16 = (w.astype(BF16) for w in (w_out, w_ff1, w_ff2))
    xs = jnp.concatenate([ctx.reshape(batch * ctx_len, d), x.reshape(batch * seq, d)], axis=0)
    tm = max(t for t in (768, 512, 256) if geo.rows % t == 0)
    tm_big = max(t for t in (1536, 768, 512, 256) if geo.rows % t == 0)
    sub0 = np.arange(geo.rows // GATE_SUB) * GATE_SUB
    sub_group = np.where(sub0 < geo.ctx_rows, 0, 1 + (sub0 - geo.ctx_rows) // seq)

    for l in range(depth):
        modt = mods[l]
        gate1_tab = modt[sub_group, 2][:, None, :]
        gate2_tab = modt[sub_group, 5][:, None, :]
        gate_row = jnp.zeros((1, 128), F32).at[0, GATE_COL0:GATE_COL0 + 4 * M_HEADS].set(
            mlstm_gate_bias[l])
        w2pad = jnp.zeros((2, 128, G_HEADS * G_DK), F32)
        for dd in range(2):
            w2pad = w2pad.at[dd, ALR_COL0 + dd * G_RANK:ALR_COL0 + (dd + 1) * G_RANK].set(
                gla_w_a2[l, dd])
        gm, gs, gg, gc = (mix_norm_g[l, i * GROUP_W:(i + 1) * GROUP_W] for i in range(N_MIXERS))

        h1 = _modulate(xs, norm1_g[l], modt, geo, shift_row=0, scale_row=1)
        z = _matmul([h1], w_in16, l, out_dtype=F32, tm=tm, tn=1280, trans_b=True, name="mm_in")
        hm_f, hm_b, hg_f, hg_b = _scan_mixers(z, gate_row, tri, w2pad.astype(BF16),
                                              gla_b_a[l].reshape(2, 1, -1), sel, msk, geo)
        y_m, y_g = _combine(hm_f, hm_b, hg_f, hg_b, z, gm, gg, geo)
        y_s = _sgu(z, sgu_ln_g[l], sgu_w[l].astype(BF16), sgu_b[l].T, gs, geo)
        y_c = _conv(z, conv_w[l], gc, geo)
        xs = _matmul([y_m, y_s, y_g, y_c], w_out16, l, out_dtype=F32, epi="resid",
                     x=xs, gate_tab=gate1_tab, tm=tm_big, tn=512, name="mm_out")
        h2 = _modulate(xs, norm2_g[l], modt, geo, shift_row=3, scale_row=4)
        hid = _matmul([h2], w_ff1_16, l, out_dtype=BF16, epi="relu2",
                      tm=tm, tn=1024, name="mm_ff1")
        xs = _matmul([hid], w_ff2_16, l, out_dtype=F32, epi="resid", x=xs,
                     gate_tab=gate2_tab, tm=tm, tn=1024, tk=4096, name="mm_ff2")

    out = _final_norm(xs, final_norm_g, geo)
    return out.reshape(batch, seq, d)
```

```python
import functools

import numpy as np
import jax
import jax.numpy as jnp
from jax import lax
from jax.experimental import pallas as pl
from jax.experimental.pallas import tpu as pltpu

F32 = jnp.float32
BF16 = jnp.bfloat16

GRID_W = 64
N_MOD = 6
N_MIXERS = 4
GROUP_W = 1024
NORM_HEAD = 256
EPS = 1e-6
NEG_INIT = -1e30

M_HEADS = 4
M_DV = GROUP_W // M_HEADS
M_DQK = M_DV // 2
M_CHUNK = 128
GATE_CAP = 15.0
S_GROUPS = 4
S_CHUNK = 128
G_HEADS = 4
G_DV = GROUP_W // G_HEADS
G_DK = G_DV // 2
G_RANK = 16
G_TAU = 16.0
G_CHUNK = 128
CONV_W = 3

_SPLIT_SIZES = (M_HEADS * M_DQK, M_HEADS * M_DQK, GROUP_W, GROUP_W, 2 * 2 * M_HEADS,
                GROUP_W, GROUP_W,
                G_HEADS * G_DK, G_HEADS * G_DK, GROUP_W, GROUP_W, 2 * G_RANK,
                GROUP_W, GROUP_W, GROUP_W)
_SPLIT_OFF = tuple(int(i) for i in np.concatenate([[0], np.cumsum(_SPLIT_SIZES)]))
_SEG_ORDER = (0, 1, 2, 3, 5, 6, 7, 8, 9, 10, 12, 13, 14, 4, 11)
Z_WIDE = 11264
Z_COLS = 11520
Z_SMALL_BLK = Z_WIDE // 128
GATE_COL0 = 0
ALR_COL0 = 2 * 2 * M_HEADS

VMEM_LIMIT = 56 * 1024 * 1024


def _cparams(sem, vmem=VMEM_LIMIT):
    return pltpu.CompilerParams(dimension_semantics=sem, vmem_limit_bytes=vmem)


def _log_sigmoid(x):
    return jnp.minimum(x, 0.0) - jnp.log1p(jnp.exp(-jnp.abs(x)))


def _split3(x):
    hi = x.astype(BF16)
    r1 = x - hi.astype(F32)
    mid = r1.astype(BF16)
    lo = (r1 - mid.astype(F32)).astype(BF16)
    return hi, mid, lo


def _dot(a, b):
    return jnp.dot(a, b, preferred_element_type=F32)


def _dot_nt(a, b):
    return lax.dot_general(a, b, (((1,), (1,)), ((), ())), preferred_element_type=F32)


def _dot_tn(a, b):
    return lax.dot_general(a, b, (((0,), (0,)), ((), ())), preferred_element_type=F32)


def _head_rms(y, g):
    outs = []
    for h in range(y.shape[-1] // NORM_HEAD):
        yh = y[:, h * NORM_HEAD:(h + 1) * NORM_HEAD]
        ms = jnp.mean(yh * yh, axis=-1, keepdims=True)
        outs.append(yh * lax.rsqrt(ms + EPS))
    return jnp.concatenate(outs, axis=-1) * g


def _mod_kernel(c_ref, a_ref, b_ref, bias_ref, o_ref, t_ref):
    @pl.when(pl.program_id(1) == 0)
    def _():
        cv = c_ref[...]
        s = cv * jax.nn.sigmoid(cv)
        t_ref[...] = _dot(s.astype(BF16), a_ref[0].astype(BF16))

    o_ref[0] = _dot(t_ref[...].astype(BF16), b_ref[0].astype(BF16)) + bias_ref[0]


def _modulation_all(cvec, mod_a, mod_b, mod_bias, tn=2048):
    depth, d, rank = mod_a.shape
    n = mod_b.shape[-1]
    return pl.pallas_call(
        _mod_kernel,
        out_shape=jax.ShapeDtypeStruct((depth, 8, n), F32),
        grid=(depth, n // tn),
        in_specs=[pl.BlockSpec((8, d), lambda l, j: (0, 0)),
                  pl.BlockSpec((1, d, rank), lambda l, j: (l, 0, 0)),
                  pl.BlockSpec((1, rank, tn), lambda l, j: (l, 0, j)),
                  pl.BlockSpec((1, 1, tn), lambda l, j: (l, 0, j))],
        out_specs=pl.BlockSpec((1, 8, tn), lambda l, j: (l, 0, j)),
        scratch_shapes=[pltpu.VMEM((8, rank), F32)],
        compiler_params=_cparams(("arbitrary", "arbitrary")),
        name="modulation",
    )(cvec, mod_a, mod_b, mod_bias.reshape(depth, 1, n))


def _modulate_kernel(x_ref, g_ref, m_ref, o_ref, *, shift_row, scale_row):
    x = x_ref[...]
    ms = jnp.mean(x * x, axis=-1, keepdims=True)
    xn = x * lax.rsqrt(ms + EPS) * g_ref[...]
    sc = m_ref[0, scale_row:scale_row + 1, :]
    sh = m_ref[0, shift_row:shift_row + 1, :]
    o_ref[...] = (xn * (1.0 + sc) + sh).astype(o_ref.dtype)


def _modulate(x, g, modt, geo, *, shift_row, scale_row):
    rows, d = x.shape
    tr = min(512, geo.ctx_rows)
    return pl.pallas_call(
        functools.partial(_modulate_kernel, shift_row=shift_row, scale_row=scale_row),
        out_shape=jax.ShapeDtypeStruct((rows, d), BF16),
        grid=(rows // tr,),
        in_specs=[pl.BlockSpec((tr, d), lambda i: (i, 0)),
                  pl.BlockSpec((1, d), lambda i: (0, 0)),
                  pl.BlockSpec((1, 8, d), lambda i: (geo.group(i * tr), 0, 0))],
        out_specs=pl.BlockSpec((tr, d), lambda i: (i, 0)),
        compiler_params=_cparams(("parallel",)),
        name="modulate",
    )(x, g.reshape(1, d), modt)


def _final_norm_kernel(x_ref, g_ref, o_ref):
    x = x_ref[...]
    ms = jnp.mean(x * x, axis=-1, keepdims=True)
    o_ref[...] = x * lax.rsqrt(ms + EPS) * g_ref[...]


def _final_norm(x, g, geo):
    rows, d = x.shape
    tr = min(512, geo.ctx_rows)
    skip = geo.ctx_rows // tr
    return pl.pallas_call(
        _final_norm_kernel,
        out_shape=jax.ShapeDtypeStruct((rows - geo.ctx_rows, d), F32),
        grid=((rows - geo.ctx_rows) // tr,),
        in_specs=[pl.BlockSpec((tr, d), lambda i: (i + skip, 0)),
                  pl.BlockSpec((1, d), lambda i: (0, 0))],
        out_specs=pl.BlockSpec((tr, d), lambda i: (i, 0)),
        compiler_params=_cparams(("parallel",)),
        name="final_norm",
    )(x, g.reshape(1, d))


def _mm_kernel(*refs, n_a, epi, nk, n_sub, trans_b, side):
    a_refs = refs[:n_a]
    b_ref = refs[n_a]
    pos = n_a + 1
    if epi == "resid":
        x_ref, g_ref = refs[pos], refs[pos + 1]
        pos += 2
    if side:
        side_in, o_ref, side_out = refs[pos], refs[pos + 1], refs[pos + 2]
        side_out[...] = side_in[...].astype(side_out.dtype)
    else:
        o_ref = refs[pos]

    part = None
    off = 0
    for a in a_refs:
        kk = a.shape[1]
        if trans_b:
            p = _dot_nt(a[...], b_ref[:, off:off + kk])
        else:
            p = _dot(a[...], b_ref[off:off + kk, :])
        part = p if part is None else part + p
        off += kk

    if epi == "relu2":
        r = jnp.maximum(part, 0.0)
        o_ref[...] = (r * r).astype(o_ref.dtype)
    elif epi == "resid":
        sub = part.shape[0] // n_sub
        gated = jnp.concatenate(
            [g_ref[r] * part[r * sub:(r + 1) * sub] for r in range(n_sub)], axis=0)
        if nk == 1:
            o_ref[...] = x_ref[...] + gated
        else:
            o_ref[...] = jnp.where(pl.program_id(2) == 0, x_ref[...], o_ref[...]) + gated
    else:
        o_ref[...] = part.astype(o_ref.dtype)


GATE_SUB = 256


SIDE_BLOCK = (256, 1024)


def _matmul(a_list, b, layer, *, out_dtype, epi="none", x=None, gate_tab=None,
            tm=768, tn=1024, tk=None, trans_b=False, side=None, name="matmul"):
    rows = a_list[0].shape[0]
    n, ktot = b.shape[1:] if trans_b else b.shape[:0:-1]
    widths = [a.shape[1] for a in a_list]
    assert sum(widths) == ktot and rows % tm == 0 and n % tn == 0 and tm % GATE_SUB == 0
    tk = ktot if tk is None else min(tk, ktot)
    nk = ktot // tk
    assert ktot % tk == 0 and (nk == 1 or (len(a_list) == 1 and epi == "resid"))
    n_sub = tm // GATE_SUB
    grid = (rows // tm, n // tn, nk)
    in_specs = []
    for w in widths:
        wk = w if nk == 1 else tk
        in_specs.append(pl.BlockSpec((tm, wk), lambda i, j, k: (i, k)))
    if trans_b:
        in_specs.append(pl.BlockSpec((None, tn, tk), lambda i, j, k: (layer, j, k)))
    else:
        in_specs.append(pl.BlockSpec((None, tk, tn), lambda i, j, k: (layer, k, j)))
    args = list(a_list) + [b]
    if epi == "resid":
        in_specs.append(pl.BlockSpec((tm, tn), lambda i, j, k: (i, j)))
        in_specs.append(pl.BlockSpec((n_sub, 1, tn), lambda i, j, k: (i, 0, j)))
        args += [x, gate_tab]
    out_shape = jax.ShapeDtypeStruct((rows, n), out_dtype)
    out_specs = pl.BlockSpec((tm, tn), lambda i, j, k: (i, j))
    if side is not None:
        w_src, l2 = side
        br, bc = SIDE_BLOCK
        while (w_src.shape[1] // br) * (w_src.shape[2] // bc) > grid[0] * grid[1] * grid[2]:
            br *= 2
        sr, sc = w_src.shape[1] // br, w_src.shape[2] // bc
        assert w_src.shape[1] % br == 0 and w_src.shape[2] % bc == 0

        def side_blk(i, j, k):
            step = jnp.minimum((i * grid[1] + j) * grid[2] + k, sr * sc - 1)
            return step // sc, step % sc

        in_specs.append(pl.BlockSpec((None, br, bc), lambda i, j, k: (l2,) + side_blk(i, j, k)))
        args.append(w_src)
        out_shape = (out_shape, jax.ShapeDtypeStruct(w_src.shape[1:], BF16))
        out_specs = (out_specs, pl.BlockSpec((br, bc), side_blk))
    return pl.pallas_call(
        functools.partial(_mm_kernel, n_a=len(a_list), epi=epi, nk=nk, n_sub=n_sub,
                          trans_b=trans_b, side=side is not None),
        out_shape=out_shape,
        grid=grid,
        in_specs=in_specs,
        out_specs=out_specs,
        compiler_params=_cparams(("parallel", "parallel", "arbitrary")),
        name=name,
    )(*args)


class _Geo:
    def __init__(self, batch, ctx_len, seq):
        self.batch, self.ctx_len, self.seq = batch, ctx_len, seq
        self.ctx_rows = batch * ctx_len
        self.rows = self.ctx_rows + batch * seq

    def group(self, row0):
        return jnp.where(row0 < self.ctx_rows, 0, 1 + (row0 - self.ctx_rows) // self.seq)

    def scan_blocks(self, chunk):
        nc, nl = self.ctx_len // chunk, self.seq // chunk
        base = self.ctx_rows // chunk

        def fwd(b, j):
            return jnp.where(j < nc, b * nc + j, base + b * nl + (j - nc))

        def bwd(b, j):
            return jnp.where(j < nc, b * nc + (nc - 1 - j), base + b * nl + (nl - 1 - (j - nc)))

        return fwd, bwd, nc + nl


def _mlstm_stages(qf, kf, vf, sf, qb, kb, vb, sb, bias_ref, tri_ref, of, ob,
                  c_ref, n_ref, m_ref):
    L = M_CHUNK
    row = lax.broadcasted_iota(jnp.int32, (L, L), 0)
    col = lax.broadcasted_iota(jnp.int32, (L, L), 1)
    streams = ((qf, kf, vf, sf, of), (qb, kb, vb, sb, ob))
    work = []
    for d, (q_ref, k_ref, v_ref, s_ref, o_ref) in enumerate(streams):
        mask = (row >= col) if d == 0 else (row <= col)
        g = s_ref[...] + bias_ref[...]
        g = GATE_CAP * jnp.tanh(g / GATE_CAP)
        lf = _log_sigmoid(g)
        hi, mid, lo = _split3(lf)
        cs = _dot(tri_ref[d], jnp.concatenate([hi, mid, lo], axis=1))
        bcum = cs[:, 0:128] + cs[:, 128:256] + cs[:, 256:384]
        bcum_t = bcum.T
        g_t = g.T
        last = L - 1 if d == 0 else 0
        for h in range(M_HEADS):
            w = dict(sidx=d * M_HEADS + h, o_ref=o_ref, h=h, mask=mask)
            ci = GATE_COL0 + d * 2 * M_HEADS + h
            cf = ci + M_HEADS
            w["b_col"], w["b_row"] = bcum[:, cf:cf + 1], bcum_t[cf:cf + 1, :]
            w["i_col"], w["i_row"] = g[:, ci:ci + 1], g_t[ci:ci + 1, :]
            w["b_tot"] = w["b_col"][last:last + 1, :]
            w["q"] = q_ref[:, h * M_DQK:(h + 1) * M_DQK] * (M_DQK ** -0.5)
            w["k"] = k_ref[:, h * M_DQK:(h + 1) * M_DQK]
            w["vb16"] = v_ref[:, h * M_DV:(h + 1) * M_DV].astype(BF16)
            work.append(w)
        yield

    for w in work:
        sidx = w["sidx"]
        w["c_st"], w["n_st"], w["m_st"] = c_ref[sidx], n_ref[sidx], m_ref[sidx][0:1, 0:1]
        w["qb16"], w["kb16"] = w["q"].astype(BF16), w["k"].astype(BF16)
        a_inter = w["b_col"] + w["m_st"]
        dmat = jnp.where(w["mask"], w["b_col"] - w["b_row"] + w["i_row"], -jnp.inf)
        w["m_t"] = jnp.maximum(a_inter, jnp.max(dmat, axis=-1, keepdims=True))
        w["w_inter"] = jnp.exp(a_inter - w["m_t"])
        w["p"] = jnp.exp(dmat - w["m_t"])
        w["qk"] = _dot_nt(w["qb16"], w["kb16"])
        w["qc"] = _dot(w["qb16"], w["c_st"].astype(BF16))
        w["qn"] = jnp.sum(w["q"] * w["n_st"], axis=-1, keepdims=True)
        yield

    for w in work:
        h = w["h"]
        s = w["qk"] * w["p"]
        num = w["w_inter"] * w["qc"] + _dot(s.astype(BF16), w["vb16"])
        den = w["w_inter"] * w["qn"] + jnp.sum(s, axis=-1, keepdims=True)
        w["o_ref"][:, h * M_DV:(h + 1) * M_DV] = (
            num / jnp.maximum(jnp.abs(den), jnp.exp(-w["m_t"])))
        yield

    for w in work:
        sidx = w["sidx"]
        g_end = w["b_tot"] - w["b_col"] + w["i_col"]
        m_new = jnp.maximum(w["b_tot"] + w["m_st"], jnp.max(g_end, axis=0, keepdims=True))
        w_old = jnp.exp(w["b_tot"] + w["m_st"] - m_new)
        kw = w["k"] * jnp.exp(g_end - m_new)
        c_ref[sidx] = w_old * w["c_st"] + _dot_tn(kw.astype(BF16), w["vb16"])
        n_ref[sidx] = w_old * w["n_st"] + jnp.sum(kw, axis=0, keepdims=True)
        m_ref[sidx] = jnp.broadcast_to(m_new, m_ref.shape[1:])
        yield


def _gla_tables(L):
    nlev = int(np.log2(L))
    sel = np.zeros((nlev + 2, L, L), np.float32)
    msk = np.zeros((nlev, L, L), np.float32)
    t = np.arange(L)
    for lev in range(nlev):
        c = 1 << lev
        mid = (t // (2 * c)) * 2 * c + c
        upper = t >= mid
        for r in range(L):
            if upper[r]:
                sel[lev, r, mid[r]:r + 1] = 1.0
            else:
                sel[lev, r, r + 1:mid[r]] = 1.0
        same = (t[:, None] // (2 * c)) == (t[None, :] // (2 * c))
        msk[lev] = (upper[:, None] & ~upper[None, :] & same).astype(np.float32)
    sel[nlev] = (t[None, :] <= t[:, None]).astype(np.float32)
    sel[nlev + 1] = (t[None, :] > t[:, None]).astype(np.float32)
    sel2 = np.stack([sel, sel[:, ::-1, ::-1]]).reshape(2, (nlev + 2) * L, L)
    msk2 = np.stack([msk, msk[:, ::-1, ::-1]])
    return sel2, msk2


def _gla_stages(qf, kf, vf, sf, qb, kb, vb, sb, w2_ref, ba_ref, sel_ref, msk_ref, of, ob,
                st_ref):
    L = G_CHUNK
    nlev = msk_ref.shape[1]
    hw = G_HEADS * G_DK
    row = lax.broadcasted_iota(jnp.int32, (L, L), 0)
    col = lax.broadcasted_iota(jnp.int32, (L, L), 1)
    eye = row == col
    streams = ((qf, kf, vf, sf, of), (qb, kb, vb, sb, ob))
    work = []
    for d, (q_ref, k_ref, v_ref, s_ref, o_ref) in enumerate(streams):
        pre = _dot(s_ref[...].astype(BF16), w2_ref[d]) + ba_ref[d]
        la = _log_sigmoid(pre) * (1.0 / G_TAU)
        hi = la.astype(BF16)
        lo = (la - hi.astype(F32)).astype(BF16)
        e2 = _dot(sel_ref[d], jnp.concatenate([hi, lo], axis=1))
        e_all = e2[:, :hw] + e2[:, hw:]
        w_lev = jnp.exp(e_all[:nlev * L])
        e_run = jnp.exp(e_all[nlev * L:(nlev + 1) * L])
        e_rem = jnp.exp(e_all[(nlev + 1) * L:])
        last = L - 1 if d == 0 else 0
        for h in range(G_HEADS):
            hs = slice(h * G_DK, (h + 1) * G_DK)
            work.append(dict(
                d=d, h=h, sidx=d * G_HEADS + h, o_ref=o_ref,
                q=q_ref[:, hs] * (G_DK ** -0.5), k=k_ref[:, hs],
                vb16=v_ref[:, h * G_DV:(h + 1) * G_DV].astype(BF16),
                w_lev=[w_lev[lev * L:(lev + 1) * L, hs] for lev in range(nlev)],
                e_run=e_run[:, hs], e_rem=e_rem[:, hs], e_last=e_run[last:last + 1, hs]))
        yield

    for w in work:
        q, k = w["q"], w["k"]
        att = jnp.where(eye, jnp.sum(q * k, axis=-1, keepdims=True), 0.0)
        for lev in range(nlev):
            wl = w["w_lev"][lev]
            a = _dot_nt((q * wl).astype(BF16), (k * wl).astype(BF16))
            att = att + a * msk_ref[w["d"], lev]
        w["att"] = att
        yield

    for w in work:
        h = w["h"]
        w["st"] = st_ref[w["sidx"]]
        w["o_ref"][:, h * G_DV:(h + 1) * G_DV] = (
            _dot_nt((w["q"] * w["e_run"]).astype(BF16), w["st"].astype(BF16))
            + _dot(w["att"].astype(BF16), w["vb16"]))
        yield

    for w in work:
        kd = (w["k"] * w["e_rem"]).astype(BF16)
        st_ref[w["sidx"]] = w["e_last"] * w["st"] + _dot_tn(w["vb16"], kd)
        yield


_DONE = object()


def _scans_kernel(mqf, mkf, mvf, gqf, gkf, gvf, sf, mqb, mkb, mvb, gqb, gkb, gvb, sb,
                  bias_ref, tri_ref, w2_ref, ba_ref, sel_ref, msk_ref,
                  mof, mob, gof, gob, c_ref, n_ref, m_ref, st_ref):
    @pl.when(pl.program_id(1) == 0)
    def _():
        c_ref[...] = jnp.zeros_like(c_ref)
        n_ref[...] = jnp.zeros_like(n_ref)
        m_ref[...] = jnp.full_like(m_ref, NEG_INIT)
        st_ref[...] = jnp.zeros_like(st_ref)

    gens = [_mlstm_stages(mqf, mkf, mvf, sf, mqb, mkb, mvb, sb, bias_ref, tri_ref, mof, mob,
                          c_ref, n_ref, m_ref),
            _gla_stages(gqf, gkf, gvf, sf, gqb, gkb, gvb, sb, w2_ref, ba_ref, sel_ref, msk_ref,
                        gof, gob, st_ref)]
    while gens:
        gens = [g for g in gens if next(g, _DONE) is not _DONE]


def _scan_mixers(z, gate_bias_row, tri, w2pad, b_a, sel, msk, geo):
    assert M_CHUNK == G_CHUNK
    L = M_CHUNK
    fwd, bwd, nsteps = geo.scan_blocks(L)
    qk_w = M_HEADS * M_DQK
    gq_blk = 5120 // qk_w
    gv_blk = 6144 // GROUP_W

    def specs(rb):
        blk = lambda w, c: pl.BlockSpec((L, w), lambda b, j: (rb(b, j), c))
        return [blk(qk_w, 0), blk(qk_w, 1), blk(GROUP_W, 1),
                blk(qk_w, gq_blk), blk(qk_w, gq_blk + 1), blk(GROUP_W, gv_blk),
                blk(128, Z_SMALL_BLK)]

    full = lambda a: pl.BlockSpec(a.shape, lambda b, j: (0,) * a.ndim)
    consts = [gate_bias_row, tri, w2pad, b_a, sel, msk]
    out = jax.ShapeDtypeStruct((geo.rows, GROUP_W), F32)
    o_f = pl.BlockSpec((L, GROUP_W), lambda b, j: (fwd(b, j), 0))
    o_b = pl.BlockSpec((L, GROUP_W), lambda b, j: (bwd(b, j), 0))
    n_str = 2 * M_HEADS
    return pl.pallas_call(
        _scans_kernel,
        out_shape=(out, out, out, out),
        grid=(geo.batch, nsteps),
        in_specs=specs(fwd) + specs(bwd) + [full(a) for a in consts],
        out_specs=(o_f, o_b, o_f, o_b),
        scratch_shapes=[pltpu.VMEM((n_str, M_DQK, M_DV), F32),
                        pltpu.VMEM((n_str, 1, M_DQK), F32),
                        pltpu.VMEM((n_str, 8, 128), F32),
                        pltpu.VMEM((2 * G_HEADS, G_DV, G_DK), F32)],
        compiler_params=_cparams(("parallel", "arbitrary")),
        name="scan_mixers",
    )(*([z] * 14), *consts)


def _sgu_chunk(u, v, ln_g, w_ref, b_ref):
    u = jax.nn.gelu(u)
    v = jax.nn.gelu(v)
    gw = GROUP_W // S_GROUPS
    outs = []
    for g in range(S_GROUPS):
        parts = []
        for h in range(g * gw, (g + 1) * gw, NORM_HEAD):
            vh = v[:, h:h + NORM_HEAD]
            vh = vh - jnp.mean(vh, axis=-1, keepdims=True)
            parts.append(vh * lax.rsqrt(jnp.mean(vh * vh, axis=-1, keepdims=True) + EPS))
        vn = (parts[0] if len(parts) == 1 else jnp.concatenate(parts, axis=-1))
        vn = vn * ln_g[:, g * gw:(g + 1) * gw]
        s = _dot(w_ref[g], vn.astype(BF16)) + b_ref[:, g:g + 1]
        outs.append(u[:, g * gw:(g + 1) * gw] * s)
    return jnp.concatenate(outs, axis=-1)


def _finish_kernel(mf, mb, gf, gb, z_og, z_u, z_v, z_sg, z_h, z_bg, z_cg,
                   mixg_ref, ln_ref, ws_ref, bs_ref, cw_ref, y_ref, *, tr, n_ctx_tiles):
    gw = GROUP_W
    gm, gs, gg, gc = (mixg_ref[:, i * gw:(i + 1) * gw] for i in range(N_MIXERS))

    hm = _head_rms(mf[...] + mb[...], gm)
    y_ref[:, 0:gw] = (hm * jax.nn.sigmoid(z_og[...])).astype(y_ref.dtype)
    zg = z_sg[...]
    hg = _head_rms(gf[...] + gb[...], gg)
    y_ref[:, 2 * gw:3 * gw] = (hg * (zg * jax.nn.sigmoid(zg))).astype(y_ref.dtype)

    for c0 in range(0, tr, S_CHUNK):
        su = _sgu_chunk(z_u[c0:c0 + S_CHUNK, :], z_v[c0:c0 + S_CHUNK, :], ln_ref[...],
                        ws_ref, bs_ref)
        y_ref[c0:c0 + S_CHUNK, gw:2 * gw] = _head_rms(su, gs).astype(y_ref.dtype)

    y = z_cg[...] * z_h[...]
    period = jnp.where(pl.program_id(0) < n_ctx_tiles, tr, GRID_W)
    pos = lax.broadcasted_iota(jnp.int32, (tr, 1), 0) & (period - 1)
    prev = jnp.where(pos != 0, pltpu.roll(y, 1, axis=0), 0.0)
    nxt = jnp.where(pos != period - 1, pltpu.roll(y, tr - 1, axis=0), 0.0)
    c = cw_ref[0:1, :] * prev + cw_ref[1:2, :] * y + cw_ref[2:3, :] * nxt
    y_ref[:, 3 * gw:4 * gw] = _head_rms(z_bg[...] * c, gc).astype(y_ref.dtype)


def _mix_finish(hm_f, hm_b, hg_f, hg_b, z, mix_g, ln_g, w_s16, b_s_t, conv_w, geo):
    tr = min(geo.ctx_len, 256)
    assert geo.ctx_len == tr and tr % GRID_W == 0 and tr % S_CHUNK == 0 and (tr & (tr - 1)) == 0
    blk = lambda c: pl.BlockSpec((tr, GROUP_W), lambda i: (i, c))
    full = lambda a: pl.BlockSpec(a.shape, lambda i: (0,) * a.ndim)
    consts = [mix_g.reshape(1, -1), ln_g.reshape(1, -1), w_s16, b_s_t, conv_w]
    return pl.pallas_call(
        functools.partial(_finish_kernel, tr=tr, n_ctx_tiles=geo.ctx_rows // tr),
        out_shape=jax.ShapeDtypeStruct((geo.rows, N_MIXERS * GROUP_W), BF16),
        grid=(geo.rows // tr,),
        in_specs=[blk(0)] * 4 + [blk(c) for c in (2, 3, 4, 7, 8, 9, 10)]
        + [full(a) for a in consts],
        out_specs=pl.BlockSpec((tr, N_MIXERS * GROUP_W), lambda i: (i, 0)),
        compiler_params=_cparams(("parallel",)),
        name="mix_finish",
    )(hm_f, hm_b, hg_f, hg_b, *([z] * 7), *consts)


def _regroup_w_in_t(w_in):
    w_t = jnp.swapaxes(w_in, 1, 2)
    segs = [w_t[:, _SPLIT_OFF[i]:_SPLIT_OFF[i + 1]] for i in _SEG_ORDER]
    pad = jnp.zeros((w_in.shape[0], Z_COLS - sum(_SPLIT_SIZES), w_in.shape[1]), w_in.dtype)
    return jnp.concatenate(segs + [pad], axis=1).astype(BF16)


def kernel(x, c, ctx, c_ctx, norm1_g, norm2_g, mod_a, mod_b, mod_bias, w_in, mlstm_gate_bias,
           gla_w_a2, gla_b_a, sgu_ln_g, sgu_w, sgu_b, conv_w, mix_norm_g, w_out, w_ff1, w_ff2,
           final_norm_g):
    batch, seq, d = x.shape
    ctx_len = ctx.shape[1]
    depth = mod_a.shape[0]
    assert ctx_len % max(M_CHUNK, G_CHUNK, S_CHUNK) == 0 and seq % 256 == 0 and d == N_MIXERS * GROUP_W
    geo = _Geo(batch, ctx_len, seq)

    cvec = jnp.zeros((8, d), F32).at[0].set(c_ctx).at[1:1 + batch].set(c)
    mods = _modulation_all(cvec, mod_a, mod_b, mod_bias)
    mods = mods.reshape(depth, 8, N_MOD, d)[:, :1 + batch]
    mods = jnp.pad(mods, ((0, 0), (0, 0), (0, 8 - N_MOD), (0, 0)))

    tri = jnp.asarray(np.stack([np.tril(np.ones((M_CHUNK, M_CHUNK), np.float32)),
                                np.triu(np.ones((M_CHUNK, M_CHUNK), np.float32))]), BF16)
    sel_np, msk_np = _gla_tables(G_CHUNK)
    sel = jnp.asarray(sel_np, BF16)
    msk = jnp.asarray(msk_np, F32)

    w_in16 = _regroup_w_in_t(w_in)
    w_out16 = w_out.astype(BF16)
    w_ff1_16, w_ff2_16 = w_ff1[:1].astype(BF16), w_ff2[:1].astype(BF16)
    xs = jnp.concatenate([ctx.reshape(batch * ctx_len, d), x.reshape(batch * seq, d)], axis=0)
    tm = max(t for t in (768, 512, 256) if geo.rows % t == 0)
    tm_big = max(t for t in (1536, 768, 512, 256) if geo.rows % t == 0)
    sub0 = np.arange(geo.rows // GATE_SUB) * GATE_SUB
    sub_group = np.where(sub0 < geo.ctx_rows, 0, 1 + (sub0 - geo.ctx_rows) // seq)

    for l in range(depth):
        modt = mods[l]
        gate1_tab = modt[sub_group, 2][:, None, :]
        gate2_tab = modt[sub_group, 5][:, None, :]
        gate_row = jnp.zeros((1, 128), F32).at[0, GATE_COL0:GATE_COL0 + 4 * M_HEADS].set(
            mlstm_gate_bias[l])
        w2pad = jnp.zeros((2, 128, G_HEADS * G_DK), F32)
        for dd in range(2):
            w2pad = w2pad.at[dd, ALR_COL0 + dd * G_RANK:ALR_COL0 + (dd + 1) * G_RANK].set(
                gla_w_a2[l, dd])
        nxt = l + 1 if l + 1 < depth else None

        h1 = _modulate(xs, norm1_g[l], modt, geo, shift_row=0, scale_row=1)
        z = _matmul([h1], w_in16, l, out_dtype=F32, tm=tm, tn=1280, trans_b=True, name="mm_in")
        hm_f, hm_b, hg_f, hg_b = _scan_mixers(z, gate_row, tri, w2pad.astype(BF16),
                                              gla_b_a[l].reshape(2, 1, -1), sel, msk, geo)
        y = _mix_finish(hm_f, hm_b, hg_f, hg_b, z, mix_norm_g[l], sgu_ln_g[l],
                        sgu_w[l].astype(BF16), sgu_b[l].T, conv_w[l], geo)
        xs = _matmul([y], w_out16, l, out_dtype=F32, epi="resid",
                     x=xs, gate_tab=gate1_tab, tm=tm_big, tn=512, name="mm_out")
        h2 = _modulate(xs, norm2_g[l], modt, geo, shift_row=3, scale_row=4)
        hid = _matmul([h2], w_ff1_16, 0, out_dtype=BF16, epi="relu2", tm=tm, tn=1024,
                      side=None if nxt is None else (w_ff1, nxt), name="mm_ff1")
        if nxt is not None:
            hid, w1_next = hid
        xs = _matmul([hid], w_ff2_16, 0, out_dtype=F32, epi="resid", x=xs, gate_tab=gate2_tab,
                     tm=tm, tn=1024, tk=4096,
                     side=None if nxt is None else (w_ff2, nxt), name="mm_ff2")
        if nxt is not None:
            xs, w2_next = xs
            w_ff1_16, w_ff2_16 = w1_next[None], w2_next[None]

    out = _final_norm(xs, final_norm_g, geo)
    return out.reshape(batch, seq, d)
```

```python
import functools

import numpy as np
import jax
import jax.numpy as jnp
from jax import lax
from jax.experimental import pallas as pl
from jax.experimental.pallas import tpu as pltpu

F32 = jnp.float32
BF16 = jnp.bfloat16

GRID_W = 64
N_MOD = 6
N_MIXERS = 4
GROUP_W = 1024
NORM_HEAD = 256
EPS = 1e-6
NEG_INIT = -1e30

M_HEADS = 4
M_DV = GROUP_W // M_HEADS
M_DQK = M_DV // 2
M_CHUNK = 128
GATE_CAP = 15.0
S_GROUPS = 4
S_CHUNK = 128
G_HEADS = 4
G_DV = GROUP_W // G_HEADS
G_DK = G_DV // 2
G_RANK = 16
G_TAU = 16.0
G_CHUNK = 128
CONV_W = 3

_SPLIT_SIZES = (M_HEADS * M_DQK, M_HEADS * M_DQK, GROUP_W, GROUP_W, 2 * 2 * M_HEADS,
                GROUP_W, GROUP_W,
                G_HEADS * G_DK, G_HEADS * G_DK, GROUP_W, GROUP_W, 2 * G_RANK,
                GROUP_W, GROUP_W, GROUP_W)
_SPLIT_OFF = tuple(int(i) for i in np.concatenate([[0], np.cumsum(_SPLIT_SIZES)]))
_SEG_ORDER = (0, 1, 2, 3, 5, 6, 7, 8, 9, 10, 12, 13, 14, 4, 11)
Z_WIDE = 11264
Z_COLS = 11520
Z_SMALL_BLK = Z_WIDE // 128
GATE_COL0 = 0
ALR_COL0 = 2 * 2 * M_HEADS

VMEM_LIMIT = 56 * 1024 * 1024


def _cparams(sem, vmem=VMEM_LIMIT):
    return pltpu.CompilerParams(dimension_semantics=sem, vmem_limit_bytes=vmem)


def _log_sigmoid(x):
    return jnp.minimum(x, 0.0) - jnp.log1p(jnp.exp(-jnp.abs(x)))


def _split3(x):
    hi = x.astype(BF16)
    r1 = x - hi.astype(F32)
    mid = r1.astype(BF16)
    lo = (r1 - mid.astype(F32)).astype(BF16)
    return hi, mid, lo


def _dot(a, b):
    return jnp.dot(a, b, preferred_element_type=F32)


def _dot_nt(a, b):
    return lax.dot_general(a, b, (((1,), (1,)), ((), ())), preferred_element_type=F32)


def _dot_tn(a, b):
    return lax.dot_general(a, b, (((0,), (0,)), ((), ())), preferred_element_type=F32)


def _head_rms(y, g):
    outs = []
    for h in range(y.shape[-1] // NORM_HEAD):
        yh = y[:, h * NORM_HEAD:(h + 1) * NORM_HEAD]
        ms = jnp.mean(yh * yh, axis=-1, keepdims=True)
        outs.append(yh * lax.rsqrt(ms + EPS))
    return jnp.concatenate(outs, axis=-1) * g


def _mod_kernel(c_ref, a_ref, b_ref, bias_ref, o_ref, t_ref):
    @pl.when(pl.program_id(1) == 0)
    def _():
        cv = c_ref[...]
        s = cv * jax.nn.sigmoid(cv)
        t_ref[...] = _dot(s.astype(BF16), a_ref[0].astype(BF16))

    o_ref[0] = _dot(t_ref[...].astype(BF16), b_ref[0].astype(BF16)) + bias_ref[0]


def _modulation_all(cvec, mod_a, mod_b, mod_bias, tn=2048):
    depth, d, rank = mod_a.shape
    n = mod_b.shape[-1]
    return pl.pallas_call(
        _mod_kernel,
        out_shape=jax.ShapeDtypeStruct((depth, 8, n), F32),
        grid=(depth, n // tn),
        in_specs=[pl.BlockSpec((8, d), lambda l, j: (0, 0)),
                  pl.BlockSpec((1, d, rank), lambda l, j: (l, 0, 0)),
                  pl.BlockSpec((1, rank, tn), lambda l, j: (l, 0, j)),
                  pl.BlockSpec((1, 1, tn), lambda l, j: (l, 0, j))],
        out_specs=pl.BlockSpec((1, 8, tn), lambda l, j: (l, 0, j)),
        scratch_shapes=[pltpu.VMEM((8, rank), F32)],
        compiler_params=_cparams(("arbitrary", "arbitrary")),
        name="modulation",
    )(cvec, mod_a, mod_b, mod_bias.reshape(depth, 1, n))


def _modulate_kernel(x_ref, g_ref, m_ref, o_ref, *, shift_row, scale_row):
    x = x_ref[...]
    ms = jnp.mean(x * x, axis=-1, keepdims=True)
    xn = x * lax.rsqrt(ms + EPS) * g_ref[...]
    sc = m_ref[0, scale_row:scale_row + 1, :]
    sh = m_ref[0, shift_row:shift_row + 1, :]
    o_ref[...] = (xn * (1.0 + sc) + sh).astype(o_ref.dtype)


def _modulate(x, g, modt, geo, *, shift_row, scale_row):
    rows, d = x.shape
    tr = min(512, geo.ctx_rows)
    return pl.pallas_call(
        functools.partial(_modulate_kernel, shift_row=shift_row, scale_row=scale_row),
        out_shape=jax.ShapeDtypeStruct((rows, d), BF16),
        grid=(rows // tr,),
        in_specs=[pl.BlockSpec((tr, d), lambda i: (i, 0)),
                  pl.BlockSpec((1, d), lambda i: (0, 0)),
                  pl.BlockSpec((1, 8, d), lambda i: (geo.group(i * tr), 0, 0))],
        out_specs=pl.BlockSpec((tr, d), lambda i: (i, 0)),
        compiler_params=_cparams(("parallel",)),
        name="modulate",
    )(x, g.reshape(1, d), modt)


def _final_norm_kernel(x_ref, g_ref, o_ref):
    x = x_ref[...]
    ms = jnp.mean(x * x, axis=-1, keepdims=True)
    o_ref[...] = x * lax.rsqrt(ms + EPS) * g_ref[...]


def _final_norm(x, g, geo):
    rows, d = x.shape
    tr = min(512, geo.ctx_rows)
    skip = geo.ctx_rows // tr
    return pl.pallas_call(
        _final_norm_kernel,
        out_shape=jax.ShapeDtypeStruct((rows - geo.ctx_rows, d), F32),
        grid=((rows - geo.ctx_rows) // tr,),
        in_specs=[pl.BlockSpec((tr, d), lambda i: (i + skip, 0)),
                  pl.BlockSpec((1, d), lambda i: (0, 0))],
        out_specs=pl.BlockSpec((tr, d), lambda i: (i, 0)),
        compiler_params=_cparams(("parallel",)),
        name="final_norm",
    )(x, g.reshape(1, d))


def _mm_kernel(*refs, n_a, epi, nk, n_sub, trans_b, side):
    a_refs = refs[:n_a]
    b_ref = refs[n_a]
    pos = n_a + 1
    if epi == "resid":
        x_ref, g_ref = refs[pos], refs[pos + 1]
        pos += 2
    if side:
        side_in, o_ref, side_out = refs[pos], refs[pos + 1], refs[pos + 2]
        side_out[...] = side_in[...].astype(side_out.dtype)
    else:
        o_ref = refs[pos]

    part = None
    off = 0
    for a in a_refs:
        kk = a.shape[1]
        if trans_b:
            p = _dot_nt(a[...], b_ref[:, off:off + kk])
        else:
            p = _dot(a[...], b_ref[off:off + kk, :])
        part = p if part is None else part + p
        off += kk

    if epi == "relu2":
        r = jnp.maximum(part, 0.0)
        o_ref[...] = (r * r).astype(o_ref.dtype)
    elif epi == "resid":
        sub = part.shape[0] // n_sub
        gated = jnp.concatenate(
            [g_ref[r] * part[r * sub:(r + 1) * sub] for r in range(n_sub)], axis=0)
        if nk == 1:
            o_ref[...] = x_ref[...] + gated
        else:
            o_ref[...] = jnp.where(pl.program_id(2) == 0, x_ref[...], o_ref[...]) + gated
    else:
        o_ref[...] = part.astype(o_ref.dtype)


GATE_SUB = 256


SIDE_BLOCK = (256, 1024)


def _matmul(a_list, b, layer, *, out_dtype, epi="none", x=None, gate_tab=None,
            tm=768, tn=1024, tk=None, trans_b=False, side=None, name="matmul"):
    rows = a_list[0].shape[0]
    n, ktot = b.shape[1:] if trans_b else b.shape[:0:-1]
    widths = [a.shape[1] for a in a_list]
    assert sum(widths) == ktot and rows % tm == 0 and n % tn == 0 and tm % GATE_SUB == 0
    tk = ktot if tk is None else min(tk, ktot)
    nk = ktot // tk
    assert ktot % tk == 0 and (nk == 1 or (len(a_list) == 1 and epi == "resid"))
    n_sub = tm // GATE_SUB
    grid = (rows // tm, n // tn, nk)
    in_specs = []
    for w in widths:
        wk = w if nk == 1 else tk
        in_specs.append(pl.BlockSpec((tm, wk), lambda i, j, k: (i, k)))
    if trans_b:
        in_specs.append(pl.BlockSpec((None, tn, tk), lambda i, j, k: (layer, j, k)))
    else:
        in_specs.append(pl.BlockSpec((None, tk, tn), lambda i, j, k: (layer, k, j)))
    args = list(a_list) + [b]
    if epi == "resid":
        in_specs.append(pl.BlockSpec((tm, tn), lambda i, j, k: (i, j)))
        in_specs.append(pl.BlockSpec((n_sub, 1, tn), lambda i, j, k: (i, 0, j)))
        args += [x, gate_tab]
    out_shape = jax.ShapeDtypeStruct((rows, n), out_dtype)
    out_specs = pl.BlockSpec((tm, tn), lambda i, j, k: (i, j))
    if side is not None:
        w_src, l2 = side
        br, bc = SIDE_BLOCK
        while (w_src.shape[1] // br) * (w_src.shape[2] // bc) > grid[0] * grid[1] * grid[2]:
            br *= 2
        sr, sc = w_src.shape[1] // br, w_src.shape[2] // bc
        assert w_src.shape[1] % br == 0 and w_src.shape[2] % bc == 0

        def side_blk(i, j, k):
            step = jnp.minimum((i * grid[1] + j) * grid[2] + k, sr * sc - 1)
            return step // sc, step % sc

        in_specs.append(pl.BlockSpec((None, br, bc), lambda i, j, k: (l2,) + side_blk(i, j, k)))
        args.append(w_src)
        out_shape = (out_shape, jax.ShapeDtypeStruct(w_src.shape[1:], BF16))
        out_specs = (out_specs, pl.BlockSpec((br, bc), side_blk))
    return pl.pallas_call(
        functools.partial(_mm_kernel, n_a=len(a_list), epi=epi, nk=nk, n_sub=n_sub,
                          trans_b=trans_b, side=side is not None),
        out_shape=out_shape,
        grid=grid,
        in_specs=in_specs,
        out_specs=out_specs,
        compiler_params=_cparams(("arbitrary",) * 3 if side is not None
                                 else ("parallel", "parallel", "arbitrary")),
        name=name,
    )(*args)


class _Geo:
    def __init__(self, batch, ctx_len, seq):
        self.batch, self.ctx_len, self.seq = batch, ctx_len, seq
        self.ctx_rows = batch * ctx_len
        self.rows = self.ctx_rows + batch * seq

    def group(self, row0):
        return jnp.where(row0 < self.ctx_rows, 0, 1 + (row0 - self.ctx_rows) // self.seq)

    def scan_blocks(self, chunk):
        nc, nl = self.ctx_len // chunk, self.seq // chunk
        base = self.ctx_rows // chunk

        def fwd(b, j):
            return jnp.where(j < nc, b * nc + j, base + b * nl + (j - nc))

        def bwd(b, j):
            return jnp.where(j < nc, b * nc + (nc - 1 - j), base + b * nl + (nl - 1 - (j - nc)))

        return fwd, bwd, nc + nl


def _mlstm_stages(qf, kf, vf, sf, qb, kb, vb, sb, bias_ref, tri_ref, of, ob,
                  c_ref, n_ref, m_ref):
    L = M_CHUNK
    row = lax.broadcasted_iota(jnp.int32, (L, L), 0)
    col = lax.broadcasted_iota(jnp.int32, (L, L), 1)
    streams = ((qf, kf, vf, sf, of), (qb, kb, vb, sb, ob))
    work = []
    for d, (q_ref, k_ref, v_ref, s_ref, o_ref) in enumerate(streams):
        mask = (row >= col) if d == 0 else (row <= col)
        g = s_ref[...] + bias_ref[...]
        g = GATE_CAP * jnp.tanh(g / GATE_CAP)
        lf = _log_sigmoid(g)
        hi, mid, lo = _split3(lf)
        cs = _dot(tri_ref[d], jnp.concatenate([hi, mid, lo], axis=1))
        bcum = cs[:, 0:128] + cs[:, 128:256] + cs[:, 256:384]
        bcum_t = bcum.T
        g_t = g.T
        last = L - 1 if d == 0 else 0
        for h in range(M_HEADS):
            w = dict(sidx=d * M_HEADS + h, o_ref=o_ref, h=h, mask=mask)
            ci = GATE_COL0 + d * 2 * M_HEADS + h
            cf = ci + M_HEADS
            w["b_col"], w["b_row"] = bcum[:, cf:cf + 1], bcum_t[cf:cf + 1, :]
            w["i_col"], w["i_row"] = g[:, ci:ci + 1], g_t[ci:ci + 1, :]
            w["b_tot"] = w["b_col"][last:last + 1, :]
            w["q"] = q_ref[:, h * M_DQK:(h + 1) * M_DQK] * (M_DQK ** -0.5)
            w["k"] = k_ref[:, h * M_DQK:(h + 1) * M_DQK]
            w["vb16"] = v_ref[:, h * M_DV:(h + 1) * M_DV].astype(BF16)
            work.append(w)
        yield

    for w in work:
        sidx = w["sidx"]
        w["c_st"], w["n_st"], w["m_st"] = c_ref[sidx], n_ref[sidx], m_ref[sidx][0:1, 0:1]
        w["qb16"], w["kb16"] = w["q"].astype(BF16), w["k"].astype(BF16)
        a_inter = w["b_col"] + w["m_st"]
        dmat = jnp.where(w["mask"], w["b_col"] - w["b_row"] + w["i_row"], -jnp.inf)
        w["m_t"] = jnp.maximum(a_inter, jnp.max(dmat, axis=-1, keepdims=True))
        w["w_inter"] = jnp.exp(a_inter - w["m_t"])
        w["p"] = jnp.exp(dmat - w["m_t"])
        w["qk"] = _dot_nt(w["qb16"], w["kb16"])
        w["qc"] = _dot(w["qb16"], w["c_st"].astype(BF16))
        w["qn"] = jnp.sum(w["q"] * w["n_st"], axis=-1, keepdims=True)
        yield

    for w in work:
        h = w["h"]
        s = w["qk"] * w["p"]
        num = w["w_inter"] * w["qc"] + _dot(s.astype(BF16), w["vb16"])
        den = w["w_inter"] * w["qn"] + jnp.sum(s, axis=-1, keepdims=True)
        w["o_ref"][:, h * M_DV:(h + 1) * M_DV] = (
            num / jnp.maximum(jnp.abs(den), jnp.exp(-w["m_t"]))).astype(w["o_ref"].dtype)
        yield

    for w in work:
        sidx = w["sidx"]
        g_end = w["b_tot"] - w["b_col"] + w["i_col"]
        m_new = jnp.maximum(w["b_tot"] + w["m_st"], jnp.max(g_end, axis=0, keepdims=True))
        w_old = jnp.exp(w["b_tot"] + w["m_st"] - m_new)
        kw = w["k"] * jnp.exp(g_end - m_new)
        c_ref[sidx] = w_old * w["c_st"] + _dot_tn(kw.astype(BF16), w["vb16"])
        n_ref[sidx] = w_old * w["n_st"] + jnp.sum(kw, axis=0, keepdims=True)
        m_ref[sidx] = jnp.broadcast_to(m_new, m_ref.shape[1:])
        yield


def _gla_tables(L):
    nlev = int(np.log2(L))
    sel = np.zeros((nlev + 2, L, L), np.float32)
    msk = np.zeros((nlev, L, L), np.float32)
    t = np.arange(L)
    for lev in range(nlev):
        c = 1 << lev
        mid = (t // (2 * c)) * 2 * c + c
        upper = t >= mid
        for r in range(L):
            if upper[r]:
                sel[lev, r, mid[r]:r + 1] = 1.0
            else:
                sel[lev, r, r + 1:mid[r]] = 1.0
        same = (t[:, None] // (2 * c)) == (t[None, :] // (2 * c))
        msk[lev] = (upper[:, None] & ~upper[None, :] & same).astype(np.float32)
    sel[nlev] = (t[None, :] <= t[:, None]).astype(np.float32)
    sel[nlev + 1] = (t[None, :] > t[:, None]).astype(np.float32)
    sel2 = np.stack([sel, sel[:, ::-1, ::-1]]).reshape(2, (nlev + 2) * L, L)
    msk2 = np.stack([msk, msk[:, ::-1, ::-1]])
    return sel2, msk2


def _gla_stages(qf, kf, vf, sf, qb, kb, vb, sb, w2_ref, ba_ref, sel_ref, msk_ref, of, ob,
                st_ref):
    L = G_CHUNK
    nlev = msk_ref.shape[1]
    hw = G_HEADS * G_DK
    row = lax.broadcasted_iota(jnp.int32, (L, L), 0)
    col = lax.broadcasted_iota(jnp.int32, (L, L), 1)
    eye = row == col
    streams = ((qf, kf, vf, sf, of), (qb, kb, vb, sb, ob))
    work = []
    for d, (q_ref, k_ref, v_ref, s_ref, o_ref) in enumerate(streams):
        pre = _dot(s_ref[...].astype(BF16), w2_ref[d]) + ba_ref[d]
        la = _log_sigmoid(pre) * (1.0 / G_TAU)
        hi = la.astype(BF16)
        lo = (la - hi.astype(F32)).astype(BF16)
        e2 = _dot(sel_ref[d], jnp.concatenate([hi, lo], axis=1))
        e_all = e2[:, :hw] + e2[:, hw:]
        w_lev = jnp.exp(e_all[:nlev * L])
        e_run = jnp.exp(e_all[nlev * L:(nlev + 1) * L])
        e_rem = jnp.exp(e_all[(nlev + 1) * L:])
        last = L - 1 if d == 0 else 0
        for h in range(G_HEADS):
            hs = slice(h * G_DK, (h + 1) * G_DK)
            work.append(dict(
                d=d, h=h, sidx=d * G_HEADS + h, o_ref=o_ref,
                q=q_ref[:, hs] * (G_DK ** -0.5), k=k_ref[:, hs],
                vb16=v_ref[:, h * G_DV:(h + 1) * G_DV].astype(BF16),
                w_lev=[w_lev[lev * L:(lev + 1) * L, hs] for lev in range(nlev)],
                e_run=e_run[:, hs], e_rem=e_rem[:, hs], e_last=e_run[last:last + 1, hs]))
        yield

    for w in work:
        q, k = w["q"], w["k"]
        att = jnp.where(eye, jnp.sum(q * k, axis=-1, keepdims=True), 0.0)
        for lev in range(nlev):
            wl = w["w_lev"][lev]
            a = _dot_nt((q * wl).astype(BF16), (k * wl).astype(BF16))
            att = att + a * msk_ref[w["d"], lev]
        w["att"] = att
        yield

    for w in work:
        h = w["h"]
        w["st"] = st_ref[w["sidx"]]
        w["o_ref"][:, h * G_DV:(h + 1) * G_DV] = (
            _dot_nt((w["q"] * w["e_run"]).astype(BF16), w["st"].astype(BF16))
            + _dot(w["att"].astype(BF16), w["vb16"])).astype(w["o_ref"].dtype)
        yield

    for w in work:
        kd = (w["k"] * w["e_rem"]).astype(BF16)
        st_ref[w["sidx"]] = w["e_last"] * w["st"] + _dot_tn(w["vb16"], kd)
        yield


_DONE = object()


def _scans_kernel(mqf, mkf, mvf, gqf, gkf, gvf, sf, mqb, mkb, mvb, gqb, gkb, gvb, sb,
                  bias_ref, tri_ref, w2_ref, ba_ref, sel_ref, msk_ref,
                  mof, mob, gof, gob, c_ref, n_ref, m_ref, st_ref):
    @pl.when(pl.program_id(1) == 0)
    def _():
        c_ref[...] = jnp.zeros_like(c_ref)
        n_ref[...] = jnp.zeros_like(n_ref)
        m_ref[...] = jnp.full_like(m_ref, NEG_INIT)
        st_ref[...] = jnp.zeros_like(st_ref)

    gens = [_mlstm_stages(mqf, mkf, mvf, sf, mqb, mkb, mvb, sb, bias_ref, tri_ref, mof, mob,
                          c_ref, n_ref, m_ref),
            _gla_stages(gqf, gkf, gvf, sf, gqb, gkb, gvb, sb, w2_ref, ba_ref, sel_ref, msk_ref,
                        gof, gob, st_ref)]
    while gens:
        gens = [g for g in gens if next(g, _DONE) is not _DONE]


def _scan_mixers(z, gate_bias_row, tri, w2pad, b_a, sel, msk, geo):
    assert M_CHUNK == G_CHUNK
    L = M_CHUNK
    fwd, bwd, nsteps = geo.scan_blocks(L)
    qk_w = M_HEADS * M_DQK
    gq_blk = 5120 // qk_w
    gv_blk = 6144 // GROUP_W

    def specs(rb):
        blk = lambda w, c: pl.BlockSpec((L, w), lambda b, j: (rb(b, j), c))
        return [blk(qk_w, 0), blk(qk_w, 1), blk(GROUP_W, 1),
                blk(qk_w, gq_blk), blk(qk_w, gq_blk + 1), blk(GROUP_W, gv_blk),
                blk(128, Z_SMALL_BLK)]

    full = lambda a: pl.BlockSpec(a.shape, lambda b, j: (0,) * a.ndim)
    consts = [gate_bias_row, tri, w2pad, b_a, sel, msk]
    out = jax.ShapeDtypeStruct((geo.rows, GROUP_W), BF16)
    o_f = pl.BlockSpec((L, GROUP_W), lambda b, j: (fwd(b, j), 0))
    o_b = pl.BlockSpec((L, GROUP_W), lambda b, j: (bwd(b, j), 0))
    n_str = 2 * M_HEADS
    return pl.pallas_call(
        _scans_kernel,
        out_shape=(out, out, out, out),
        grid=(geo.batch, nsteps),
        in_specs=specs(fwd) + specs(bwd) + [full(a) for a in consts],
        out_specs=(o_f, o_b, o_f, o_b),
        scratch_shapes=[pltpu.VMEM((n_str, M_DQK, M_DV), F32),
                        pltpu.VMEM((n_str, 1, M_DQK), F32),
                        pltpu.VMEM((n_str, 8, 128), F32),
                        pltpu.VMEM((2 * G_HEADS, G_DV, G_DK), F32)],
        compiler_params=_cparams(("parallel", "arbitrary")),
        name="scan_mixers",
    )(*([z] * 14), *consts)


def _sgu_chunk(u, v, ln_g, w_ref, b_ref):
    u = jax.nn.gelu(u)
    v = jax.nn.gelu(v)
    gw = GROUP_W // S_GROUPS
    outs = []
    for g in range(S_GROUPS):
        parts = []
        for h in range(g * gw, (g + 1) * gw, NORM_HEAD):
            vh = v[:, h:h + NORM_HEAD]
            vh = vh - jnp.mean(vh, axis=-1, keepdims=True)
            parts.append(vh * lax.rsqrt(jnp.mean(vh * vh, axis=-1, keepdims=True) + EPS))
        vn = (parts[0] if len(parts) == 1 else jnp.concatenate(parts, axis=-1))
        vn = vn * ln_g[:, g * gw:(g + 1) * gw]
        s = _dot(w_ref[g], vn.astype(BF16)) + b_ref[:, g:g + 1]
        outs.append(u[:, g * gw:(g + 1) * gw] * s)
    return jnp.concatenate(outs, axis=-1)


def _finish_kernel(mf, mb, gf, gb, z_og, z_u, z_v, z_sg, z_h, z_bg, z_cg,
                   mixg_ref, ln_ref, ws_ref, bs_ref, cw_ref, y_ref, *, tr, n_ctx_tiles):
    gw = GROUP_W
    gm, gs, gg, gc = (mixg_ref[:, i * gw:(i + 1) * gw] for i in range(N_MIXERS))

    hm = _head_rms(mf[...].astype(F32) + mb[...].astype(F32), gm)
    y_ref[:, 0:gw] = (hm * jax.nn.sigmoid(z_og[...])).astype(y_ref.dtype)
    zg = z_sg[...]
    hg = _head_rms(gf[...].astype(F32) + gb[...].astype(F32), gg)
    y_ref[:, 2 * gw:3 * gw] = (hg * (zg * jax.nn.sigmoid(zg))).astype(y_ref.dtype)

    for c0 in range(0, tr, S_CHUNK):
        su = _sgu_chunk(z_u[c0:c0 + S_CHUNK, :], z_v[c0:c0 + S_CHUNK, :], ln_ref[...],
                        ws_ref, bs_ref)
        y_ref[c0:c0 + S_CHUNK, gw:2 * gw] = _head_rms(su, gs).astype(y_ref.dtype)

    y = z_cg[...] * z_h[...]
    period = jnp.where(pl.program_id(0) < n_ctx_tiles, tr, GRID_W)
    pos = lax.broadcasted_iota(jnp.int32, (tr, 1), 0) & (period - 1)
    prev = jnp.where(pos != 0, pltpu.roll(y, 1, axis=0), 0.0)
    nxt = jnp.where(pos != period - 1, pltpu.roll(y, tr - 1, axis=0), 0.0)
    c = cw_ref[0:1, :] * prev + cw_ref[1:2, :] * y + cw_ref[2:3, :] * nxt
    y_ref[:, 3 * gw:4 * gw] = _head_rms(z_bg[...] * c, gc).astype(y_ref.dtype)


def _mix_finish(hm_f, hm_b, hg_f, hg_b, z, mix_g, ln_g, w_s16, b_s_t, conv_w, geo):
    tr = min(geo.ctx_len, 256)
    assert geo.ctx_len == tr and tr % GRID_W == 0 and tr % S_CHUNK == 0 and (tr & (tr - 1)) == 0
    blk = lambda c: pl.BlockSpec((tr, GROUP_W), lambda i: (i, c))
    full = lambda a: pl.BlockSpec(a.shape, lambda i: (0,) * a.ndim)
    consts = [mix_g.reshape(1, -1), ln_g.reshape(1, -1), w_s16, b_s_t, conv_w]
    return pl.pallas_call(
        functools.partial(_finish_kernel, tr=tr, n_ctx_tiles=geo.ctx_rows // tr),
        out_shape=jax.ShapeDtypeStruct((geo.rows, N_MIXERS * GROUP_W), BF16),
        grid=(geo.rows // tr,),
        in_specs=[blk(0)] * 4 + [blk(c) for c in (2, 3, 4, 7, 8, 9, 10)]
        + [full(a) for a in consts],
        out_specs=pl.BlockSpec((tr, N_MIXERS * GROUP_W), lambda i: (i, 0)),
        compiler_params=_cparams(("parallel",)),
        name="mix_finish",
    )(hm_f, hm_b, hg_f, hg_b, *([z] * 7), *consts)


def _regroup_w_in_t(w_in):
    w_t = jnp.swapaxes(w_in, 1, 2)
    segs = [w_t[:, _SPLIT_OFF[i]:_SPLIT_OFF[i + 1]] for i in _SEG_ORDER]
    pad = jnp.zeros((w_in.shape[0], Z_COLS - sum(_SPLIT_SIZES), w_in.shape[1]), w_in.dtype)
    return jnp.concatenate(segs + [pad], axis=1).astype(BF16)


def kernel(x, c, ctx, c_ctx, norm1_g, norm2_g, mod_a, mod_b, mod_bias, w_in, mlstm_gate_bias,
           gla_w_a2, gla_b_a, sgu_ln_g, sgu_w, sgu_b, conv_w, mix_norm_g, w_out, w_ff1, w_ff2,
           final_norm_g):
    batch, seq, d = x.shape
    ctx_len = ctx.shape[1]
    depth = mod_a.shape[0]
    assert ctx_len % max(M_CHUNK, G_CHUNK, S_CHUNK) == 0 and seq % 256 == 0 and d == N_MIXERS * GROUP_W
    geo = _Geo(batch, ctx_len, seq)

    cvec = jnp.zeros((8, d), F32).at[0].set(c_ctx).at[1:1 + batch].set(c)
    mods = _modulation_all(cvec, mod_a, mod_b, mod_bias)
    mods = mods.reshape(depth, 8, N_MOD, d)[:, :1 + batch]
    mods = jnp.pad(mods, ((0, 0), (0, 0), (0, 8 - N_MOD), (0, 0)))

    tri = jnp.asarray(np.stack([np.tril(np.ones((M_CHUNK, M_CHUNK), np.float32)),
                                np.triu(np.ones((M_CHUNK, M_CHUNK), np.float32))]), BF16)
    sel_np, msk_np = _gla_tables(G_CHUNK)
    sel = jnp.asarray(sel_np, BF16)
    msk = jnp.asarray(msk_np, F32)

    w_in16 = _regroup_w_in_t(w_in)
    w_out16 = w_out[:1].astype(BF16)
    xs = jnp.concatenate([ctx.reshape(batch * ctx_len, d), x.reshape(batch * seq, d)], axis=0)
    tm = max(t for t in (768, 512, 256) if geo.rows % t == 0)
    tm_big = max(t for t in (1536, 768, 512, 256) if geo.rows % t == 0)
    sub0 = np.arange(geo.rows // GATE_SUB) * GATE_SUB
    sub_group = np.where(sub0 < geo.ctx_rows, 0, 1 + (sub0 - geo.ctx_rows) // seq)

    for l in range(depth):
        modt = mods[l]
        gate1_tab = modt[sub_group, 2][:, None, :]
        gate2_tab = modt[sub_group, 5][:, None, :]
        gate_row = jnp.zeros((1, 128), F32).at[0, GATE_COL0:GATE_COL0 + 4 * M_HEADS].set(
            mlstm_gate_bias[l])
        w2pad = jnp.zeros((2, 128, G_HEADS * G_DK), F32)
        for dd in range(2):
            w2pad = w2pad.at[dd, ALR_COL0 + dd * G_RANK:ALR_COL0 + (dd + 1) * G_RANK].set(
                gla_w_a2[l, dd])
        nxt = l + 1 if l + 1 < depth else None

        h1 = _modulate(xs, norm1_g[l], modt, geo, shift_row=0, scale_row=1)
        z, w_ff1_16 = _matmul([h1], w_in16, l, out_dtype=F32, tm=tm, tn=1280, trans_b=True,
                              side=(w_ff1, l), name="mm_in")
        hm_f, hm_b, hg_f, hg_b = _scan_mixers(z, gate_row, tri, w2pad.astype(BF16),
                                              gla_b_a[l].reshape(2, 1, -1), sel, msk, geo)
        y = _mix_finish(hm_f, hm_b, hg_f, hg_b, z, mix_norm_g[l], sgu_ln_g[l],
                        sgu_w[l].astype(BF16), sgu_b[l].T, conv_w[l], geo)
        xs = _matmul([y], w_out16, 0, out_dtype=F32, epi="resid",
                     x=xs, gate_tab=gate1_tab, tm=tm_big, tn=512, name="mm_out")
        h2 = _modulate(xs, norm2_g[l], modt, geo, shift_row=3, scale_row=4)
        hid, w_ff2_16 = _matmul([h2], w_ff1_16[None], 0, out_dtype=BF16, epi="relu2", tm=tm,
                                tn=1024, side=(w_ff2, l), name="mm_ff1")
        xs = _matmul([hid], w_ff2_16[None], 0, out_dtype=F32, epi="resid", x=xs,
                     gate_tab=gate2_tab, tm=tm, tn=1024, tk=4096,
                     side=None if nxt is None else (w_out, nxt), name="mm_ff2")
        if nxt is not None:
            xs, w_out_next = xs
            w_out16 = w_out_next[None]

    out = _final_norm(xs, final_norm_g, geo)
    return out.reshape(batch, seq, d)
```

```python
import functools

import numpy as np
import jax
import jax.numpy as jnp
from jax import lax
from jax.experimental import pallas as pl
from jax.experimental.pallas import tpu as pltpu

F32 = jnp.float32
BF16 = jnp.bfloat16

GRID_W = 64
N_MOD = 6
N_MIXERS = 4
GROUP_W = 1024
NORM_HEAD = 256
EPS = 1e-6
NEG_INIT = -1e30

M_HEADS = 4
M_DV = GROUP_W // M_HEADS
M_DQK = M_DV // 2
M_CHUNK = 128
GATE_CAP = 15.0
S_GROUPS = 4
S_CHUNK = 128
G_HEADS = 4
G_DV = GROUP_W // G_HEADS
G_DK = G_DV // 2
G_RANK = 16
G_TAU = 16.0
G_CHUNK = 128
CONV_W = 3

_SPLIT_SIZES = (M_HEADS * M_DQK, M_HEADS * M_DQK, GROUP_W, GROUP_W, 2 * 2 * M_HEADS,
                GROUP_W, GROUP_W,
                G_HEADS * G_DK, G_HEADS * G_DK, GROUP_W, GROUP_W, 2 * G_RANK,
                GROUP_W, GROUP_W, GROUP_W)
_SPLIT_OFF = tuple(int(i) for i in np.concatenate([[0], np.cumsum(_SPLIT_SIZES)]))
_SEG_ORDER = (0, 1, 2, 3, 5, 6, 7, 8, 9, 10, 12, 13, 14, 4, 11)
Z_WIDE = 11264
Z_COLS = 11520
Z_SMALL_BLK = Z_WIDE // 128
GATE_COL0 = 0
ALR_COL0 = 2 * 2 * M_HEADS

VMEM_LIMIT = 56 * 1024 * 1024


def _cparams(sem, vmem=VMEM_LIMIT):
    return pltpu.CompilerParams(dimension_semantics=sem, vmem_limit_bytes=vmem)


def _log_sigmoid(x):
    return jnp.minimum(x, 0.0) - jnp.log1p(jnp.exp(-jnp.abs(x)))


def _split3(x):
    hi = x.astype(BF16)
    r1 = x - hi.astype(F32)
    mid = r1.astype(BF16)
    lo = (r1 - mid.astype(F32)).astype(BF16)
    return hi, mid, lo


def _dot(a, b):
    return jnp.dot(a, b, preferred_element_type=F32)


def _dot_nt(a, b):
    return lax.dot_general(a, b, (((1,), (1,)), ((), ())), preferred_element_type=F32)


def _dot_tn(a, b):
    return lax.dot_general(a, b, (((0,), (0,)), ((), ())), preferred_element_type=F32)


def _head_rms(y, g):
    outs = []
    for h in range(y.shape[-1] // NORM_HEAD):
        yh = y[:, h * NORM_HEAD:(h + 1) * NORM_HEAD]
        ms = jnp.mean(yh * yh, axis=-1, keepdims=True)
        outs.append(yh * lax.rsqrt(ms + EPS))
    return jnp.concatenate(outs, axis=-1) * g


def _mod_kernel(c_ref, a_ref, b_ref, bias_ref, o_ref, t_ref):
    @pl.when(pl.program_id(1) == 0)
    def _():
        cv = c_ref[...]
        s = cv * jax.nn.sigmoid(cv)
        t_ref[...] = _dot(s.astype(BF16), a_ref[0].astype(BF16))

    o_ref[0] = _dot(t_ref[...].astype(BF16), b_ref[0].astype(BF16)) + bias_ref[0]


def _modulation_all(cvec, mod_a, mod_b, mod_bias, tn=2048):
    depth, d, rank = mod_a.shape
    n = mod_b.shape[-1]
    return pl.pallas_call(
        _mod_kernel,
        out_shape=jax.ShapeDtypeStruct((depth, 8, n), F32),
        grid=(depth, n // tn),
        in_specs=[pl.BlockSpec((8, d), lambda l, j: (0, 0)),
                  pl.BlockSpec((1, d, rank), lambda l, j: (l, 0, 0)),
                  pl.BlockSpec((1, rank, tn), lambda l, j: (l, 0, j)),
                  pl.BlockSpec((1, 1, tn), lambda l, j: (l, 0, j))],
        out_specs=pl.BlockSpec((1, 8, tn), lambda l, j: (l, 0, j)),
        scratch_shapes=[pltpu.VMEM((8, rank), F32)],
        compiler_params=_cparams(("arbitrary", "arbitrary")),
        name="modulation",
    )(cvec, mod_a, mod_b, mod_bias.reshape(depth, 1, n))


def _modulate_kernel(x_ref, g_ref, m_ref, o_ref, *, shift_row, scale_row):
    x = x_ref[...]
    ms = jnp.mean(x * x, axis=-1, keepdims=True)
    xn = x * lax.rsqrt(ms + EPS) * g_ref[...]
    sc = m_ref[0, scale_row:scale_row + 1, :]
    sh = m_ref[0, shift_row:shift_row + 1, :]
    o_ref[...] = (xn * (1.0 + sc) + sh).astype(o_ref.dtype)


def _modulate(x, g, modt, geo, *, shift_row, scale_row):
    rows, d = x.shape
    tr = min(512, geo.ctx_rows)
    return pl.pallas_call(
        functools.partial(_modulate_kernel, shift_row=shift_row, scale_row=scale_row),
        out_shape=jax.ShapeDtypeStruct((rows, d), BF16),
        grid=(rows // tr,),
        in_specs=[pl.BlockSpec((tr, d), lambda i: (i, 0)),
                  pl.BlockSpec((1, d), lambda i: (0, 0)),
                  pl.BlockSpec((1, 8, d), lambda i: (geo.group(i * tr), 0, 0))],
        out_specs=pl.BlockSpec((tr, d), lambda i: (i, 0)),
        compiler_params=_cparams(("parallel",)),
        name="modulate",
    )(x, g.reshape(1, d), modt)


def _final_norm_kernel(x_ref, g_ref, o_ref):
    x = x_ref[...]
    ms = jnp.mean(x * x, axis=-1, keepdims=True)
    o_ref[...] = x * lax.rsqrt(ms + EPS) * g_ref[...]


def _final_norm(x, g, geo):
    rows, d = x.shape
    tr = min(512, geo.ctx_rows)
    skip = geo.ctx_rows // tr
    return pl.pallas_call(
        _final_norm_kernel,
        out_shape=jax.ShapeDtypeStruct((rows - geo.ctx_rows, d), F32),
        grid=((rows - geo.ctx_rows) // tr,),
        in_specs=[pl.BlockSpec((tr, d), lambda i: (i + skip, 0)),
                  pl.BlockSpec((1, d), lambda i: (0, 0))],
        out_specs=pl.BlockSpec((tr, d), lambda i: (i, 0)),
        compiler_params=_cparams(("parallel",)),
        name="final_norm",
    )(x, g.reshape(1, d))


def _mm_kernel(*refs, n_a, epi, nk, n_sub, trans_b, side, mod, prev):
    a_refs = refs[:n_a]
    b_ref = refs[n_a]
    pos = n_a + 1
    if epi == "resid":
        x_ref, g_ref = refs[pos], refs[pos + 1]
        pos += 2
    if side:
        side_in = refs[pos]
        pos += 1
    if mod:
        mx_ref, mg_ref, mm_ref = refs[pos:pos + 3]
        pos += 3
    if prev:
        pos += 1
    o_ref = refs[pos]
    pos += 1
    if side:
        side_out = refs[pos]
        pos += 1
        side_out[...] = side_in[...].astype(side_out.dtype)
    if mod:
        mh_ref = refs[pos]

    part = None
    off = 0
    for a in a_refs:
        kk = a.shape[1]
        if trans_b:
            p = _dot_nt(a[...], b_ref[:, off:off + kk])
        else:
            p = _dot(a[...], b_ref[off:off + kk, :])
        part = p if part is None else part + p
        off += kk

    if epi == "relu2":
        r = jnp.maximum(part, 0.0)
        o_ref[...] = (r * r).astype(o_ref.dtype)
    elif epi == "resid":
        sub = part.shape[0] // n_sub
        gated = jnp.concatenate(
            [g_ref[r] * part[r * sub:(r + 1) * sub] for r in range(n_sub)], axis=0)
        if nk == 1:
            o_ref[...] = x_ref[...] + gated
        else:
            o_ref[...] = jnp.where(pl.program_id(2) == 0, x_ref[...], o_ref[...]) + gated
    else:
        o_ref[...] = part.astype(o_ref.dtype)

    if mod:
        shift_row, scale_row = mod
        xr = mx_ref[...]
        ms = jnp.mean(xr * xr, axis=-1, keepdims=True)
        gain = mg_ref[...] * (1.0 + mm_ref[0, scale_row:scale_row + 1, :])
        mh_ref[...] = (xr * lax.rsqrt(ms + EPS) * gain
                       + mm_ref[0, shift_row:shift_row + 1, :]).astype(mh_ref.dtype)


GATE_SUB = 256


SIDE_BLOCK = (256, 1024)


def _matmul(a_list, b, layer, *, out_dtype, epi="none", x=None, gate_tab=None,
            tm=768, tn=1024, tk=None, trans_b=False, side=None, name="matmul",
            row_tiles=None, a_full=True, out_rows=None, prev_out=None, mod=None):
    n, ktot = b.shape[1:] if trans_b else b.shape[:0:-1]
    widths = [a.shape[1] for a in a_list]
    out_rows = a_list[0].shape[0] if out_rows is None else out_rows
    t0, nt = (0, out_rows // tm) if row_tiles is None else row_tiles
    a0 = t0 if a_full else 0
    assert sum(widths) == ktot and out_rows % tm == 0 and n % tn == 0 and tm % GATE_SUB == 0
    tk = ktot if tk is None else min(tk, ktot)
    nk = ktot // tk
    assert ktot % tk == 0 and (nk == 1 or (len(a_list) == 1 and epi == "resid"))
    n_sub = tm // GATE_SUB
    grid = (nt, n // tn, nk)
    n_steps = grid[0] * grid[1] * grid[2]
    step_of = lambda i, j, k: (i * grid[1] + j) * grid[2] + k
    in_specs = []
    for w in widths:
        wk = w if nk == 1 else tk
        in_specs.append(pl.BlockSpec((tm, wk), lambda i, j, k: (i + a0, k)))
    if trans_b:
        in_specs.append(pl.BlockSpec((None, tn, tk), lambda i, j, k: (layer, j, k)))
    else:
        in_specs.append(pl.BlockSpec((None, tk, tn), lambda i, j, k: (layer, k, j)))
    args = list(a_list) + [b]
    if epi == "resid":
        in_specs.append(pl.BlockSpec((tm, tn), lambda i, j, k: (i + t0, j)))
        in_specs.append(pl.BlockSpec((n_sub, 1, tn), lambda i, j, k: (i + t0, 0, j)))
        args += [x, gate_tab]
    out_shape = [jax.ShapeDtypeStruct((out_rows, n), out_dtype)]
    out_specs = [pl.BlockSpec((tm, tn), lambda i, j, k: (i + t0, j))]
    aliases = {}
    if side is not None:
        w_src, l2 = side
        br, bc = SIDE_BLOCK
        while (w_src.shape[1] // br) * (w_src.shape[2] // bc) > n_steps:
            br *= 2
        sr, sc = w_src.shape[1] // br, w_src.shape[2] // bc
        assert w_src.shape[1] % br == 0 and w_src.shape[2] % bc == 0

        def side_blk(i, j, k):
            step = jnp.minimum(step_of(i, j, k), sr * sc - 1)
            return step // sc, step % sc

        in_specs.append(pl.BlockSpec((None, br, bc), lambda i, j, k: (l2,) + side_blk(i, j, k)))
        args.append(w_src)
        out_shape.append(jax.ShapeDtypeStruct(w_src.shape[1:], BF16))
        out_specs.append(pl.BlockSpec((br, bc), side_blk))
    if mod is not None:
        geo, d = mod["geo"], mod["xs"].shape[1]
        blk = 64
        while mod["nrows"] // blk > n_steps:
            blk *= 2
        assert blk <= GATE_SUB and mod["nrows"] % blk == 0 and mod["row0"] % blk == 0
        nblk, blk0 = mod["nrows"] // blk, mod["row0"] // blk
        mblk = lambda i, j, k: jnp.minimum(step_of(i, j, k), nblk - 1)
        if mod["alias"]:
            aliases[len(args)] = 0
        in_specs += [pl.BlockSpec((blk, d), lambda i, j, k: (blk0 + mblk(i, j, k), 0)),
                     pl.BlockSpec((1, d), lambda i, j, k: (0, 0)),
                     pl.BlockSpec((1, 8, d), lambda i, j, k: (
                         geo.group((blk0 + mblk(i, j, k)) * blk), 0, 0))]
        args += [mod["xs"], mod["g"].reshape(1, d), mod["modt"]]
        out_shape.append(jax.ShapeDtypeStruct((mod["nrows"], d), BF16))
        out_specs.append(pl.BlockSpec((blk, d), lambda i, j, k: (mblk(i, j, k), 0)))
    if prev_out is not None:
        aliases[len(args)] = 0
        in_specs.append(pl.BlockSpec(memory_space=pl.ANY))
        args.append(prev_out)
    single = len(out_shape) == 1
    return pl.pallas_call(
        functools.partial(_mm_kernel, n_a=len(a_list), epi=epi, nk=nk, n_sub=n_sub,
                          trans_b=trans_b, side=side is not None,
                          mod=None if mod is None else (mod["shift_row"], mod["scale_row"]),
                          prev=prev_out is not None),
        out_shape=out_shape[0] if single else tuple(out_shape),
        grid=grid,
        in_specs=in_specs,
        out_specs=out_specs[0] if single else tuple(out_specs),
        input_output_aliases=aliases,
        compiler_params=_cparams(("parallel", "parallel", "arbitrary") if single
                                 else ("arbitrary",) * 3),
        name=name,
    )(*args)


class _Geo:
    def __init__(self, batch, ctx_len, seq):
        self.batch, self.ctx_len, self.seq = batch, ctx_len, seq
        self.ctx_rows = batch * ctx_len
        self.rows = self.ctx_rows + batch * seq

    def group(self, row0):
        return jnp.where(row0 < self.ctx_rows, 0, 1 + (row0 - self.ctx_rows) // self.seq)

    def scan_blocks(self, chunk):
        nc, nl = self.ctx_len // chunk, self.seq // chunk
        base = self.ctx_rows // chunk

        def fwd(b, j):
            return jnp.where(j < nc, b * nc + j, base + b * nl + (j - nc))

        def bwd(b, j):
            return jnp.where(j < nc, b * nc + (nc - 1 - j), base + b * nl + (nl - 1 - (j - nc)))

        return fwd, bwd, nc + nl


def _mlstm_stages(qf, kf, vf, sf, qb, kb, vb, sb, bias_ref, tri_ref, of, ob,
                  c_ref, n_ref, m_ref):
    L = M_CHUNK
    row = lax.broadcasted_iota(jnp.int32, (L, L), 0)
    col = lax.broadcasted_iota(jnp.int32, (L, L), 1)
    streams = ((qf, kf, vf, sf, of), (qb, kb, vb, sb, ob))
    work = []
    for d, (q_ref, k_ref, v_ref, s_ref, o_ref) in enumerate(streams):
        mask = (row >= col) if d == 0 else (row <= col)
        g = s_ref[...] + bias_ref[...]
        g = GATE_CAP * jnp.tanh(g / GATE_CAP)
        lf = _log_sigmoid(g)
        hi, mid, lo = _split3(lf)
        cs = _dot(tri_ref[d], jnp.concatenate([hi, mid, lo], axis=1))
        bcum = cs[:, 0:128] + cs[:, 128:256] + cs[:, 256:384]
        bcum_t = bcum.T
        g_t = g.T
        last = L - 1 if d == 0 else 0
        for h in range(M_HEADS):
            w = dict(sidx=d * M_HEADS + h, o_ref=o_ref, h=h, mask=mask)
            ci = GATE_COL0 + d * 2 * M_HEADS + h
            cf = ci + M_HEADS
            w["b_col"], w["b_row"] = bcum[:, cf:cf + 1], bcum_t[cf:cf + 1, :]
            w["i_col"], w["i_row"] = g[:, ci:ci + 1], g_t[ci:ci + 1, :]
            w["b_tot"] = w["b_col"][last:last + 1, :]
            w["q"] = q_ref[:, h * M_DQK:(h + 1) * M_DQK] * (M_DQK ** -0.5)
            w["k"] = k_ref[:, h * M_DQK:(h + 1) * M_DQK]
            w["vb16"] = v_ref[:, h * M_DV:(h + 1) * M_DV].astype(BF16)
            work.append(w)
        yield

    for w in work:
        sidx = w["sidx"]
        w["c_st"], w["n_st"], w["m_st"] = c_ref[sidx], n_ref[sidx], m_ref[sidx][0:1, 0:1]
        w["qb16"], w["kb16"] = w["q"].astype(BF16), w["k"].astype(BF16)
        a_inter = w["b_col"] + w["m_st"]
        dmat = jnp.where(w["mask"], w["b_col"] - w["b_row"] + w["i_row"], -jnp.inf)
        w["m_t"] = jnp.maximum(a_inter, jnp.max(dmat, axis=-1, keepdims=True))
        w["w_inter"] = jnp.exp(a_inter - w["m_t"])
        w["p"] = jnp.exp(dmat - w["m_t"])
        w["qk"] = _dot_nt(w["qb16"], w["kb16"])
        w["qc"] = _dot(w["qb16"], w["c_st"].astype(BF16))
        w["qn"] = jnp.sum(w["q"] * w["n_st"], axis=-1, keepdims=True)
        yield

    for w in work:
        h = w["h"]
        s = w["qk"] * w["p"]
        num = w["w_inter"] * w["qc"] + _dot(s.astype(BF16), w["vb16"])
        den = w["w_inter"] * w["qn"] + jnp.sum(s, axis=-1, keepdims=True)
        w["o_ref"][:, h * M_DV:(h + 1) * M_DV] = (
            num / jnp.maximum(jnp.abs(den), jnp.exp(-w["m_t"]))).astype(w["o_ref"].dtype)
        yield

    for w in work:
        sidx = w["sidx"]
        g_end = w["b_tot"] - w["b_col"] + w["i_col"]
        m_new = jnp.maximum(w["b_tot"] + w["m_st"], jnp.max(g_end, axis=0, keepdims=True))
        w_old = jnp.exp(w["b_tot"] + w["m_st"] - m_new)
        kw = w["k"] * jnp.exp(g_end - m_new)
        c_ref[sidx] = w_old * w["c_st"] + _dot_tn(kw.astype(BF16), w["vb16"])
        n_ref[sidx] = w_old * w["n_st"] + jnp.sum(kw, axis=0, keepdims=True)
        m_ref[sidx] = jnp.broadcast_to(m_new, m_ref.shape[1:])
        yield


def _gla_tables(L):
    nlev = int(np.log2(L))
    sel = np.zeros((nlev + 2, L, L), np.float32)
    msk = np.zeros((nlev, L, L), np.float32)
    t = np.arange(L)
    for lev in range(nlev):
        c = 1 << lev
        mid = (t // (2 * c)) * 2 * c + c
        upper = t >= mid
        for r in range(L):
            if upper[r]:
                sel[lev, r, mid[r]:r + 1] = 1.0
            else:
                sel[lev, r, r + 1:mid[r]] = 1.0
        same = (t[:, None] // (2 * c)) == (t[None, :] // (2 * c))
        msk[lev] = (upper[:, None] & ~upper[None, :] & same).astype(np.float32)
    sel[nlev] = (t[None, :] <= t[:, None]).astype(np.float32)
    sel[nlev + 1] = (t[None, :] > t[:, None]).astype(np.float32)
    sel2 = np.stack([sel, sel[:, ::-1, ::-1]]).reshape(2, (nlev + 2) * L, L)
    msk2 = np.stack([msk, msk[:, ::-1, ::-1]])
    return sel2, msk2


def _gla_stages(qf, kf, vf, sf, qb, kb, vb, sb, w2_ref, ba_ref, sel_ref, msk_ref, of, ob,
                st_ref):
    L = G_CHUNK
    nlev = msk_ref.shape[1]
    hw = G_HEADS * G_DK
    row = lax.broadcasted_iota(jnp.int32, (L, L), 0)
    col = lax.broadcasted_iota(jnp.int32, (L, L), 1)
    eye = row == col
    streams = ((qf, kf, vf, sf, of), (qb, kb, vb, sb, ob))
    work = []
    for d, (q_ref, k_ref, v_ref, s_ref, o_ref) in enumerate(streams):
        pre = _dot(s_ref[...].astype(BF16), w2_ref[d]) + ba_ref[d]
        la = _log_sigmoid(pre) * (1.0 / G_TAU)
        hi = la.astype(BF16)
        lo = (la - hi.astype(F32)).astype(BF16)
        e2 = _dot(sel_ref[d], jnp.concatenate([hi, lo], axis=1))
        e_all = e2[:, :hw] + e2[:, hw:]
        w_lev = jnp.exp(e_all[:nlev * L])
        e_run = jnp.exp(e_all[nlev * L:(nlev + 1) * L])
        e_rem = jnp.exp(e_all[(nlev + 1) * L:])
        last = L - 1 if d == 0 else 0
        for h in range(G_HEADS):
            hs = slice(h * G_DK, (h + 1) * G_DK)
            work.append(dict(
                d=d, h=h, sidx=d * G_HEADS + h, o_ref=o_ref,
                q=q_ref[:, hs] * (G_DK ** -0.5), k=k_ref[:, hs],
                vb16=v_ref[:, h * G_DV:(h + 1) * G_DV].astype(BF16),
                w_lev=[w_lev[lev * L:(lev + 1) * L, hs] for lev in range(nlev)],
                e_run=e_run[:, hs], e_rem=e_rem[:, hs], e_last=e_run[last:last + 1, hs]))
        yield

    for w in work:
        q, k = w["q"], w["k"]
        att = jnp.where(eye, jnp.sum(q * k, axis=-1, keepdims=True), 0.0)
        for lev in range(nlev):
            wl = w["w_lev"][lev]
            a = _dot_nt((q * wl).astype(BF16), (k * wl).astype(BF16))
            att = att + a * msk_ref[w["d"], lev]
        w["att"] = att
        yield

    for w in work:
        h = w["h"]
        w["st"] = st_ref[w["sidx"]]
        w["o_ref"][:, h * G_DV:(h + 1) * G_DV] = (
            _dot_nt((w["q"] * w["e_run"]).astype(BF16), w["st"].astype(BF16))
            + _dot(w["att"].astype(BF16), w["vb16"])).astype(w["o_ref"].dtype)
        yield

    for w in work:
        kd = (w["k"] * w["e_rem"]).astype(BF16)
        st_ref[w["sidx"]] = w["e_last"] * w["st"] + _dot_tn(w["vb16"], kd)
        yield


_DONE = object()


def _scans_kernel(mqf, mkf, mvf, gqf, gkf, gvf, sf, mqb, mkb, mvb, gqb, gkb, gvb, sb,
                  bias_ref, tri_ref, w2_ref, ba_ref, sel_ref, msk_ref,
                  mof, mob, gof, gob, c_ref, n_ref, m_ref, st_ref):
    @pl.when(pl.program_id(1) == 0)
    def _():
        c_ref[...] = jnp.zeros_like(c_ref)
        n_ref[...] = jnp.zeros_like(n_ref)
        m_ref[...] = jnp.full_like(m_ref, NEG_INIT)
        st_ref[...] = jnp.zeros_like(st_ref)

    gens = [_mlstm_stages(mqf, mkf, mvf, sf, mqb, mkb, mvb, sb, bias_ref, tri_ref, mof, mob,
                          c_ref, n_ref, m_ref),
            _gla_stages(gqf, gkf, gvf, sf, gqb, gkb, gvb, sb, w2_ref, ba_ref, sel_ref, msk_ref,
                        gof, gob, st_ref)]
    while gens:
        gens = [g for g in gens if next(g, _DONE) is not _DONE]


def _scan_mixers(z, gate_bias_row, tri, w2pad, b_a, sel, msk, geo):
    assert M_CHUNK == G_CHUNK
    L = M_CHUNK
    fwd, bwd, nsteps = geo.scan_blocks(L)
    qk_w = M_HEADS * M_DQK
    gq_blk = 5120 // qk_w
    gv_blk = 6144 // GROUP_W

    def specs(rb):
        blk = lambda w, c: pl.BlockSpec((L, w), lambda b, j: (rb(b, j), c))
        return [blk(qk_w, 0), blk(qk_w, 1), blk(GROUP_W, 1),
                blk(qk_w, gq_blk), blk(qk_w, gq_blk + 1), blk(GROUP_W, gv_blk),
                blk(128, Z_SMALL_BLK)]

    full = lambda a: pl.BlockSpec(a.shape, lambda b, j: (0,) * a.ndim)
    consts = [gate_bias_row, tri, w2pad, b_a, sel, msk]
    out = jax.ShapeDtypeStruct((geo.rows, GROUP_W), BF16)
    o_f = pl.BlockSpec((L, GROUP_W), lambda b, j: (fwd(b, j), 0))
    o_b = pl.BlockSpec((L, GROUP_W), lambda b, j: (bwd(b, j), 0))
    n_str = 2 * M_HEADS
    return pl.pallas_call(
        _scans_kernel,
        out_shape=(out, out, out, out),
        grid=(geo.batch, nsteps),
        in_specs=specs(fwd) + specs(bwd) + [full(a) for a in consts],
        out_specs=(o_f, o_b, o_f, o_b),
        scratch_shapes=[pltpu.VMEM((n_str, M_DQK, M_DV), F32),
                        pltpu.VMEM((n_str, 1, M_DQK), F32),
                        pltpu.VMEM((n_str, 8, 128), F32),
                        pltpu.VMEM((2 * G_HEADS, G_DV, G_DK), F32)],
        compiler_params=_cparams(("parallel", "arbitrary")),
        name="scan_mixers",
    )(*([z] * 14), *consts)


def _sgu_chunk(u, v, ln_g, w_ref, b_ref):
    u = jax.nn.gelu(u)
    v = jax.nn.gelu(v)
    gw = GROUP_W // S_GROUPS
    outs = []
    for g in range(S_GROUPS):
        parts = []
        for h in range(g * gw, (g + 1) * gw, NORM_HEAD):
            vh = v[:, h:h + NORM_HEAD]
            vh = vh - jnp.mean(vh, axis=-1, keepdims=True)
            parts.append(vh * lax.rsqrt(jnp.mean(vh * vh, axis=-1, keepdims=True) + EPS))
        vn = (parts[0] if len(parts) == 1 else jnp.concatenate(parts, axis=-1))
        vn = vn * ln_g[:, g * gw:(g + 1) * gw]
        s = _dot(w_ref[g], vn.astype(BF16)) + b_ref[:, g:g + 1]
        outs.append(u[:, g * gw:(g + 1) * gw] * s)
    return jnp.concatenate(outs, axis=-1)


def _finish_kernel(mf, mb, gf, gb, z_og, z_u, z_v, z_sg, z_h, z_bg, z_cg,
                   mixg_ref, ln_ref, ws_ref, bs_ref, cw_ref, y_ref, *, tr, n_ctx_tiles):
    gw = GROUP_W
    gm, gs, gg, gc = (mixg_ref[:, i * gw:(i + 1) * gw] for i in range(N_MIXERS))

    hm = _head_rms(mf[...].astype(F32) + mb[...].astype(F32), gm)
    y_ref[:, 0:gw] = (hm * jax.nn.sigmoid(z_og[...])).astype(y_ref.dtype)
    zg = z_sg[...]
    hg = _head_rms(gf[...].astype(F32) + gb[...].astype(F32), gg)
    y_ref[:, 2 * gw:3 * gw] = (hg * (zg * jax.nn.sigmoid(zg))).astype(y_ref.dtype)

    for c0 in range(0, tr, S_CHUNK):
        su = _sgu_chunk(z_u[c0:c0 + S_CHUNK, :], z_v[c0:c0 + S_CHUNK, :], ln_ref[...],
                        ws_ref, bs_ref)
        y_ref[c0:c0 + S_CHUNK, gw:2 * gw] = _head_rms(su, gs).astype(y_ref.dtype)

    y = z_cg[...] * z_h[...]
    period = jnp.where(pl.program_id(0) < n_ctx_tiles, tr, GRID_W)
    pos = lax.broadcasted_iota(jnp.int32, (tr, 1), 0) & (period - 1)
    prev = jnp.where(pos != 0, pltpu.roll(y, 1, axis=0), 0.0)
    nxt = jnp.where(pos != period - 1, pltpu.roll(y, tr - 1, axis=0), 0.0)
    c = cw_ref[0:1, :] * prev + cw_ref[1:2, :] * y + cw_ref[2:3, :] * nxt
    y_ref[:, 3 * gw:4 * gw] = _head_rms(z_bg[...] * c, gc).astype(y_ref.dtype)


def _mix_finish(hm_f, hm_b, hg_f, hg_b, z, mix_g, ln_g, w_s16, b_s_t, conv_w, geo):
    tr = min(geo.ctx_len, 256)
    assert geo.ctx_len == tr and tr % GRID_W == 0 and tr % S_CHUNK == 0 and (tr & (tr - 1)) == 0
    blk = lambda c: pl.BlockSpec((tr, GROUP_W), lambda i: (i, c))
    full = lambda a: pl.BlockSpec(a.shape, lambda i: (0,) * a.ndim)
    consts = [mix_g.reshape(1, -1), ln_g.reshape(1, -1), w_s16, b_s_t, conv_w]
    return pl.pallas_call(
        functools.partial(_finish_kernel, tr=tr, n_ctx_tiles=geo.ctx_rows // tr),
        out_shape=jax.ShapeDtypeStruct((geo.rows, N_MIXERS * GROUP_W), BF16),
        grid=(geo.rows // tr,),
        in_specs=[blk(0)] * 4 + [blk(c) for c in (2, 3, 4, 7, 8, 9, 10)]
        + [full(a) for a in consts],
        out_specs=pl.BlockSpec((tr, N_MIXERS * GROUP_W), lambda i: (i, 0)),
        compiler_params=_cparams(("parallel",)),
        name="mix_finish",
    )(hm_f, hm_b, hg_f, hg_b, *([z] * 7), *consts)


def _regroup_w_in_t(w_in):
    w_t = jnp.swapaxes(w_in, 1, 2)
    segs = [w_t[:, _SPLIT_OFF[i]:_SPLIT_OFF[i + 1]] for i in _SEG_ORDER]
    pad = jnp.zeros((w_in.shape[0], Z_COLS - sum(_SPLIT_SIZES), w_in.shape[1]), w_in.dtype)
    return jnp.concatenate(segs + [pad], axis=1).astype(BF16)


def kernel(x, c, ctx, c_ctx, norm1_g, norm2_g, mod_a, mod_b, mod_bias, w_in, mlstm_gate_bias,
           gla_w_a2, gla_b_a, sgu_ln_g, sgu_w, sgu_b, conv_w, mix_norm_g, w_out, w_ff1, w_ff2,
           final_norm_g):
    batch, seq, d = x.shape
    ctx_len = ctx.shape[1]
    depth = mod_a.shape[0]
    assert ctx_len % max(M_CHUNK, G_CHUNK, S_CHUNK) == 0 and seq % 256 == 0 and d == N_MIXERS * GROUP_W
    geo = _Geo(batch, ctx_len, seq)

    cvec = jnp.zeros((8, d), F32).at[0].set(c_ctx).at[1:1 + batch].set(c)
    mods = _modulation_all(cvec, mod_a, mod_b, mod_bias)
    mods = mods.reshape(depth, 8, N_MOD, d)[:, :1 + batch]
    mods = jnp.pad(mods, ((0, 0), (0, 0), (0, 8 - N_MOD), (0, 0)))

    tri = jnp.asarray(np.stack([np.tril(np.ones((M_CHUNK, M_CHUNK), np.float32)),
                                np.triu(np.ones((M_CHUNK, M_CHUNK), np.float32))]), BF16)
    sel_np, msk_np = _gla_tables(G_CHUNK)
    sel = jnp.asarray(sel_np, BF16)
    msk = jnp.asarray(msk_np, F32)

    w_in16 = _regroup_w_in_t(w_in)
    w_out16 = w_out[:1].astype(BF16)
    xs = jnp.concatenate([ctx.reshape(batch * ctx_len, d), x.reshape(batch * seq, d)], axis=0)
    tm = max(t for t in (768, 512, 256) if geo.rows % t == 0)
    half = geo.rows // 2
    half_tiles = half // tm
    assert half % tm == 0 and half % GATE_SUB == 0
    sub0 = np.arange(geo.rows // GATE_SUB) * GATE_SUB
    sub_group = np.where(sub0 < geo.ctx_rows, 0, 1 + (sub0 - geo.ctx_rows) // seq)

    for l in range(depth):
        modt = mods[l]
        gate1_tab = modt[sub_group, 2][:, None, :]
        gate2_tab = modt[sub_group, 5][:, None, :]
        gate_row = jnp.zeros((1, 128), F32).at[0, GATE_COL0:GATE_COL0 + 4 * M_HEADS].set(
            mlstm_gate_bias[l])
        w2pad = jnp.zeros((2, 128, G_HEADS * G_DK), F32)
        for dd in range(2):
            w2pad = w2pad.at[dd, ALR_COL0 + dd * G_RANK:ALR_COL0 + (dd + 1) * G_RANK].set(
                gla_w_a2[l, dd])
        nxt = l + 1 if l + 1 < depth else None

        h1 = _modulate(xs, norm1_g[l], modt, geo, shift_row=0, scale_row=1)
        z, w_ff1_16 = _matmul([h1], w_in16, l, out_dtype=F32, tm=tm, tn=1280, trans_b=True,
                              side=(w_ff1, l), name="mm_in")
        hm_f, hm_b, hg_f, hg_b = _scan_mixers(z, gate_row, tri, w2pad.astype(BF16),
                                              gla_b_a[l].reshape(2, 1, -1), sel, msk, geo)
        y = _mix_finish(hm_f, hm_b, hg_f, hg_b, z, mix_norm_g[l], sgu_ln_g[l],
                        sgu_w[l].astype(BF16), sgu_b[l].T, conv_w[l], geo)
        mod2 = dict(g=norm2_g[l], modt=modt, geo=geo, shift_row=3, scale_row=4, nrows=half)
        mm_out = functools.partial(_matmul, [y], w_out16, 0, out_dtype=F32, epi="resid", x=xs,
                                   gate_tab=gate1_tab, tm=tm, tn=512, name="mm_out")
        xs_a = mm_out(row_tiles=(0, half_tiles))
        xs, h2_a = mm_out(row_tiles=(half_tiles, half_tiles),
                          mod=dict(mod2, xs=xs_a, row0=0, alias=True))
        mm_ff1 = functools.partial(_matmul, b=w_ff1_16[None], layer=0, out_dtype=BF16,
                                   epi="relu2", tm=tm, tn=1024, a_full=False,
                                   out_rows=geo.rows, name="mm_ff1")
        hid_a, h2_b = mm_ff1([h2_a], row_tiles=(0, half_tiles),
                             mod=dict(mod2, xs=xs, row0=half, alias=False))
        hid, w_ff2_16 = mm_ff1([h2_b], row_tiles=(half_tiles, half_tiles), prev_out=hid_a,
                               side=(w_ff2, l))
        xs = _matmul([hid], w_ff2_16[None], 0, out_dtype=F32, epi="resid", x=xs,
                     gate_tab=gate2_tab, tm=tm, tn=1024, tk=4096,
                     side=None if nxt is None else (w_out, nxt), name="mm_ff2")
        if nxt is not None:
            xs, w_out_next = xs
            w_out16 = w_out_next[None]

    out = _final_norm(xs, final_norm_g, geo)
    return out.reshape(batch, seq, d)
```
